```python
import math
import jax, jax.numpy as jnp
from jax import lax
import numpy as np

D_MODEL = 2048
BATCH = 2
SEQ = 4096
DEPTH = 4
DEC_BATCH = 8
DEC_SEQ = 8
PAST_LEN = 16384
PAGE_SIZE = 128

HEAD_DIM = 128
MIX_W = D_MODEL
H_RET = MIX_W // HEAD_DIM // 2
H_MOBA = MIX_W // HEAD_DIM - H_RET
RET_W = H_RET * HEAD_DIM
MOBA_W = H_MOBA * HEAD_DIM
IN_W = 4 * RET_W + 3 * MOBA_W
IN_SPLITS = (RET_W, 2 * RET_W, 3 * RET_W, 4 * RET_W, 4 * RET_W + MOBA_W, 4 * RET_W + 2 * MOBA_W)
RET_CHUNK = 128
MOBA_BLOCK = 256
MOBA_TOPK = 3
MOBA_Q_BLOCK = 16
POOL_WINDOWS = (2, 4, 8, 16)
POOL_GROUP_W = D_MODEL // len(POOL_WINDOWS)
POOL_STATE = max(POOL_WINDOWS) - 1
D_FF = 4 * D_MODEL
N_EVEN = (DEPTH + 1) // 2
N_ODD = DEPTH // 2
EPS = 1e-6

kernel_name = "hybrid_retention_moba_pool_decoder_step"


def rms_norm(x, g):
    xf = x.astype(jnp.float32)
    y = xf * lax.rsqrt(jnp.mean(xf * xf, axis=-1, keepdims=True) + EPS) * g.astype(jnp.float32)
    return y.astype(x.dtype)


def sq_relu_mlp(h, w_up, w_down):
    u = jnp.einsum('btd,df->btf', h, w_up)
    return jnp.einsum('btf,fd->btd', jnp.square(jax.nn.relu(u)), w_down)


def retention_chunk(q, k, v, s, lg):
    L = q.shape[1]
    idx = jnp.arange(L, dtype=jnp.float32)
    diff = idx[:, None] - idx[None, :]
    decay = jnp.where(diff >= 0, jnp.exp(lg[:, None, None] * jnp.maximum(diff, 0.0)), 0.0)
    inner = jnp.einsum('bihd,bjhd->bhij', q, k) * decay[None]
    o = jnp.einsum('bhij,bjhe->bihe', inner, v)
    cross = jnp.exp(lg[None, :] * (idx[:, None] + 1.0))
    o = o + jnp.einsum('bihd,bhde->bihe', q, s) * cross[None, :, :, None]
    wk = jnp.exp(lg[None, :] * (L - 1.0 - idx[:, None]))
    s_new = jnp.exp(lg * L)[None, :, None, None] * s + jnp.einsum('bjhd,bjhe->bhde', k * wk[None, :, :, None], v)
    return o, s_new


def retention_scan(q, k, v, s0, lg):
    B, T, H, D = q.shape
    c = math.gcd(T, RET_CHUNK)
    n = T // c

    def to_chunks(a):
        return a.reshape(B, n, c, H, a.shape[-1]).swapaxes(0, 1)

    def step(s, qkv):
        o, s = retention_chunk(qkv[0], qkv[1], qkv[2], s, lg)
        return s, o

    s_fin, o = lax.scan(step, s0, (to_chunks(q), to_chunks(k), to_chunks(v)))
    return o.swapaxes(0, 1).reshape(B, T, H, -1), s_fin


def moba_attend(q, q_pos, k_all, v_all, slopes):
    B, T, H, D = q.shape
    nb = k_all.shape[1] // MOBA_BLOCK
    kb = k_all.reshape(B, nb, MOBA_BLOCK, H, D)
    vb = v_all.reshape(B, nb, MOBA_BLOCK, H, D)
    kmean = jnp.mean(kb.astype(jnp.float32), axis=2)
    kk = min(MOBA_TOPK, nb)
    qb = math.gcd(T, MOBA_Q_BLOCK)
    nq = T // qb
    qs = q.reshape(B, nq, qb, H, D).transpose(1, 0, 3, 2, 4)
    ps = q_pos.reshape(nq, qb)
    bi = jnp.arange(B)[:, None, None, None]
    hi = jnp.arange(H)[None, :, None, None]
    offs = jnp.arange(MOBA_BLOCK)
    scale = D ** -0.5

    def one(args):
        qc, pc = args
        qf = qc.astype(jnp.float32)
        own = pc // MOBA_BLOCK
        bscore = jnp.einsum('bhqd,bnhd->bhqn', qf, kmean)
        past = jnp.arange(nb)[None, :] < own[:, None]
        bscore = jnp.where(past[None, None], bscore, -jnp.inf)
        _, sel = lax.top_k(bscore, kk)
        sel_ok = sel < own[None, None, :, None]
        own_b = jnp.broadcast_to(own[None, None, :, None], (B, H, qb, 1))
        blocks = jnp.concatenate([sel, own_b], axis=-1)
        ok = jnp.concatenate([sel_ok, jnp.ones((B, H, qb, 1), bool)], axis=-1)
        kg = kb[bi, blocks, :, hi].astype(jnp.float32)
        vg = vb[bi, blocks, :, hi].astype(jnp.float32)
        kpos = blocks[..., None] * MOBA_BLOCK + offs
        dist = (pc[None, None, :, None, None] - kpos)
        s = jnp.einsum('bhqd,bhqnkd->bhqnk', qf, kg) * scale
        s = s - slopes[None, :, None, None, None] * dist.astype(jnp.float32)
        s = jnp.where(ok[..., None] & (dist >= 0), s, -jnp.inf)
        p = jax.nn.softmax(s.reshape(B, H, qb, -1), axis=-1).reshape(s.shape)
        return jnp.einsum('bhqnk,bhqnkd->bhqd', p, vg)

    out = lax.map(one, (qs, ps))
    return out.transpose(1, 0, 3, 2, 4).reshape(B, T, H, D)


def even_mixer(h, w_in, w_out, s0, k_past, v_past, pos, lg, slopes):
    B, T, _ = h.shape
    z = jnp.einsum('btd,de->bte', h, w_in)
    qr, kr, vr, gr, qm, km, vm = jnp.split(z, IN_SPLITS, axis=-1)

    def heads(a):
        return a.reshape(B, T, -1, HEAD_DIM)

    f32 = jnp.float32
    o_r, s_new = retention_scan(heads(qr).astype(f32), heads(kr).astype(f32) * (HEAD_DIM ** -0.5),
                                heads(vr).astype(f32), s0.astype(f32), lg)
    mu = jnp.mean(o_r, axis=-1, keepdims=True)
    var = jnp.mean(jnp.square(o_r - mu), axis=-1, keepdims=True)
    o_r = (o_r - mu) * lax.rsqrt(var + EPS)
    o_r = (o_r.reshape(B, T, RET_W) * jax.nn.silu(gr.astype(f32))).astype(h.dtype)

    k_new = heads(km)
    v_new = heads(vm)
    if k_past is None:
        k_all, v_all = k_new, v_new
    else:
        k_all = jnp.concatenate([k_past.astype(k_new.dtype), k_new], axis=1)
        v_all = jnp.concatenate([v_past.astype(v_new.dtype), v_new], axis=1)
    pad = (-k_all.shape[1]) % MOBA_BLOCK
    k_all = jnp.pad(k_all, ((0, 0), (0, pad), (0, 0), (0, 0)))
    v_all = jnp.pad(v_all, ((0, 0), (0, pad), (0, 0), (0, 0)))
    o_b = moba_attend(heads(qm), pos, k_all, v_all, slopes).reshape(B, T, MOBA_W).astype(h.dtype)

    m = jnp.einsum('bte,ed->btd', jnp.concatenate([o_r, o_b], axis=-1), w_out)
    return m, s_new, k_new, v_new


def pool_mixer(h, prefix, w_pool, scale):
    B, T, _ = h.shape
    x_ext = h if prefix is None else jnp.concatenate([prefix.astype(h.dtype), h], axis=1)
    L = x_ext.shape[1]
    P = L - T
    xf = x_ext.astype(jnp.float32)
    cs = jnp.pad(lax.cumsum(xf, axis=1), ((0, 0), (1, 0), (0, 0)))
    t = jnp.arange(P, L)
    groups = []
    for gi, w in enumerate(POOL_WINDOWS):
        sl = slice(gi * POOL_GROUP_W, (gi + 1) * POOL_GROUP_W)
        lo = jnp.maximum(t + 1 - w, 0)
        cnt = (t + 1 - lo).astype(jnp.float32)
        csg = cs[:, :, sl]
        mean = (csg[:, t + 1] - csg[:, lo]) / cnt[None, :, None]
        groups.append(mean - xf[:, P:, sl])
    pooled = jnp.stack(groups, axis=2)
    y = jnp.einsum('btgc,gce->btge', pooled, w_pool.astype(jnp.float32)).reshape(B, T, D_MODEL)
    y = (y * scale.astype(jnp.float32)).astype(h.dtype)
    return y, x_ext[:, L - POOL_STATE:]


def trunk(x, pos, ret0, pool0, paged, w_in, w_out, w_pool, pool_scale, w_up, w_down, norm_g, lg, slopes):
    B = x.shape[0]
    new_k, new_v, new_ret, new_pool = [], [], [], []
    for l in range(DEPTH):
        h = rms_norm(x, norm_g[l, 0])
        if l % 2 == 0:
            e = l // 2
            if paged is None:
                k_past = None
                v_past = None
            else:
                cache_k, cache_v, page_table = paged
                k_past = cache_k[e][page_table].reshape(B, -1, H_MOBA, HEAD_DIM)
                v_past = cache_v[e][page_table].reshape(B, -1, H_MOBA, HEAD_DIM)
            m, s_new, k_new, v_new = even_mixer(h, w_in[e], w_out[e], ret0[e], k_past, v_past, pos, lg, slopes)
            new_k.append(k_new)
            new_v.append(v_new)
            new_ret.append(s_new)
        else:
            o = l // 2
            m, p_new = pool_mixer(h, None if pool0 is None else pool0[o], w_pool[o], pool_scale[o])
            new_pool.append(p_new)
        x = x + rms_norm(m, norm_g[l, 1])
        h = rms_norm(x, norm_g[l, 2])
        x = x + rms_norm(sq_relu_mlp(h, w_up[l], w_down[l]), norm_g[l, 3])
    return x, jnp.stack(new_k), jnp.stack(new_v), jnp.stack(new_ret), jnp.stack(new_pool)


def setup_inputs(seed: int = 0) -> dict:
    key = jax.random.key(seed)
    ks = jax.random.split(key, 16)
    f32 = jnp.float32
    n_pages = PAST_LEN // PAGE_SIZE
    pool_pages = (5 * DEC_BATCH * n_pages) // 4
    x_prompt = jax.random.normal(ks[0], (BATCH, SEQ, D_MODEL), f32)
    x_sample = jax.random.normal(ks[1], (DEC_BATCH, DEC_SEQ, D_MODEL), f32)
    cache_k = jax.random.normal(ks[2], (N_EVEN, pool_pages, PAGE_SIZE, H_MOBA, HEAD_DIM), f32)
    cache_v = jax.random.normal(ks[3], (N_EVEN, pool_pages, PAGE_SIZE, H_MOBA, HEAD_DIM), f32)
    state_ret = 0.5 * jax.random.normal(ks[4], (N_EVEN, DEC_BATCH, H_RET, HEAD_DIM, HEAD_DIM), f32)
    state_pool = jax.random.normal(ks[5], (N_ODD, DEC_BATCH, POOL_STATE, D_MODEL), f32)
    page_table = jax.random.permutation(ks[6], pool_pages)[:DEC_BATCH * n_pages]
    page_table = page_table.reshape(DEC_BATCH, n_pages).astype(jnp.int32)
    w_in = jax.random.normal(ks[7], (N_EVEN, D_MODEL, IN_W), f32) * D_MODEL ** -0.5
    w_out = jax.random.normal(ks[8], (N_EVEN, MIX_W, D_MODEL), f32) * MIX_W ** -0.5
    w_pool = jax.random.normal(ks[9], (N_ODD, len(POOL_WINDOWS), POOL_GROUP_W, POOL_GROUP_W), f32) * POOL_GROUP_W ** -0.5
    pool_scale = 1.0 + 0.1 * jax.random.normal(ks[10], (N_ODD, D_MODEL), f32)
    w_up = jax.random.normal(ks[11], (DEPTH, D_MODEL, D_FF), f32) * D_MODEL ** -0.5
    w_down = jax.random.normal(ks[12], (DEPTH, D_FF, D_MODEL), f32) * D_FF ** -0.5
    norm_g = 1.0 + 0.05 * jax.random.normal(ks[13], (DEPTH, 4, D_MODEL), f32)
    return {"x_prompt": x_prompt, "x_sample": x_sample, "cache_k": cache_k, "cache_v": cache_v,
            "state_ret": state_ret, "state_pool": state_pool, "page_table": page_table,
            "w_in": w_in, "w_out": w_out, "w_pool": w_pool, "pool_scale": pool_scale,
            "w_up": w_up, "w_down": w_down, "norm_g": norm_g}


def reference(x_prompt, x_sample, cache_k, cache_v, state_ret, state_pool, page_table,
              w_in, w_out, w_pool, pool_scale, w_up, w_down, norm_g):
    lg = jnp.log1p(-jnp.exp2(-5.0 - jnp.arange(H_RET, dtype=jnp.float32)))
    slopes = jnp.exp2(-8.0 * (jnp.arange(H_MOBA, dtype=jnp.float32) + 1.0) / H_MOBA)
    past_len = page_table.shape[1] * PAGE_SIZE
    prompt_pos = jnp.arange(x_prompt.shape[1], dtype=jnp.int32)
    sample_pos = past_len + jnp.arange(x_sample.shape[1], dtype=jnp.int32)
    ret_zero = jnp.zeros((N_EVEN, x_prompt.shape[0], H_RET, HEAD_DIM, HEAD_DIM), jnp.float32)

    y_prompt, k_prompt, v_prompt, ret_prompt, pool_prompt = trunk(
        x_prompt, prompt_pos, ret_zero, None, None,
        w_in, w_out, w_pool, pool_scale, w_up, w_down, norm_g, lg, slopes)
    y_sample, k_sample, v_sample, ret_sample, pool_sample = trunk(
        x_sample, sample_pos, state_ret, state_pool, (cache_k, cache_v, page_table),
        w_in, w_out, w_pool, pool_scale, w_up, w_down, norm_g, lg, slopes)
    return (y_prompt, y_sample, k_prompt, v_prompt, k_sample, v_sample,
            ret_prompt, ret_sample, pool_prompt, pool_sample)
```

```python
import functools

import jax
import jax.numpy as jnp
from jax import lax
from jax.experimental import pallas as pl
from jax.experimental.pallas import tpu as pltpu

F32 = jnp.float32
BF16 = jnp.bfloat16

HEAD_DIM = 128
MOBA_BLOCK = 256
MOBA_TOPK = 3
POOL_WINDOWS = (2, 4, 8, 16)
POOL_HALO = 16
POOL_STATE = max(POOL_WINDOWS) - 1
EPS = 1e-6
NEG = -1e30
VMEM_LIMIT = 48 * 1024 * 1024

_NT = (((1,), (1,)), ((), ()))
_TN = (((0,), (0,)), ((), ()))


def _params(*sem):
    return pltpu.CompilerParams(dimension_semantics=sem, vmem_limit_bytes=VMEM_LIMIT)


def _tile(n, target, align):
    if n <= target:
        return n
    t = target - target % align
    while n % t:
        t -= align
    return t


def _rms(x, g):
    return x * lax.rsqrt(jnp.mean(x * x, axis=-1, keepdims=True) + EPS) * g


def _norm_matmul_kernel(x_ref, g_ref, w_ref, o_ref, h_ref):
    @pl.when(pl.program_id(1) == 0)
    def _():
        h_ref[...] = _rms(x_ref[...], g_ref[...]).astype(BF16)

    o_ref[...] = jnp.dot(h_ref[...], w_ref[...], preferred_element_type=F32)


def _norm_matmul(x, g, w, tm, tn):
    m, d = x.shape
    n = w.shape[1]
    tm, tn = _tile(m, tm, 8), _tile(n, tn, 128)
    return pl.pallas_call(
        _norm_matmul_kernel,
        out_shape=jax.ShapeDtypeStruct((m, n), F32),
        grid=(m // tm, n // tn),
        in_specs=[pl.BlockSpec((tm, d), lambda i, j: (i, 0)),
                  pl.BlockSpec((1, d), lambda i, j: (0, 0)),
                  pl.BlockSpec((d, tn), lambda i, j: (0, j))],
        out_specs=pl.BlockSpec((tm, tn), lambda i, j: (i, j)),
        scratch_shapes=[pltpu.VMEM((tm, d), BF16)],
        compiler_params=_params("parallel", "arbitrary"),
        name="norm_matmul",
    )(x, g.reshape(1, d), w)


def _mlp_kernel(x_ref, g2_ref, g3_ref, wu_ref, wd_ref, o_ref, h_ref, acc_ref):
    f = pl.program_id(1)

    @pl.when(f == 0)
    def _():
        h_ref[...] = _rms(x_ref[...], g2_ref[...]).astype(BF16)
        acc_ref[...] = jnp.zeros_like(acc_ref)

    u = jnp.dot(h_ref[...], wu_ref[...], preferred_element_type=F32)
    a = jnp.square(jnp.maximum(u, 0.0)).astype(BF16)
    acc_ref[...] += jnp.dot(a, wd_ref[...], preferred_element_type=F32)

    @pl.when(f == pl.num_programs(1) - 1)
    def _():
        o_ref[...] = x_ref[...] + _rms(acc_ref[...], g3_ref[...])


def _mlp(x, g2, g3, w_up, w_down, tm, tf):
    m, d = x.shape
    ff = w_up.shape[1]
    tm, tf = _tile(m, tm, 8), _tile(ff, tf, 128)
    return pl.pallas_call(
        _mlp_kernel,
        out_shape=jax.ShapeDtypeStruct((m, d), F32),
        grid=(m // tm, ff // tf),
        in_specs=[pl.BlockSpec((tm, d), lambda i, f: (i, 0)),
                  pl.BlockSpec((1, d), lambda i, f: (0, 0)),
                  pl.BlockSpec((1, d), lambda i, f: (0, 0)),
                  pl.BlockSpec((d, tf), lambda i, f: (0, f)),
                  pl.BlockSpec((tf, d), lambda i, f: (f, 0))],
        out_specs=pl.BlockSpec((tm, d), lambda i, f: (i, 0)),
        scratch_shapes=[pltpu.VMEM((tm, d), BF16), pltpu.VMEM((tm, d), F32)],
        compiler_params=_params("parallel", "arbitrary"),
        name="mlp",
    )(x, g2.reshape(1, d), g3.reshape(1, d), w_up, w_down)


def _outproj_kernel(or_ref, ob_ref, w_ref, x_ref, g_ref, o_ref, *, ret_w):
    m = jnp.dot(or_ref[...].astype(BF16), w_ref[0:ret_w, :], preferred_element_type=F32)
    m = m + jnp.dot(ob_ref[...].astype(BF16), w_ref[ret_w:, :], preferred_element_type=F32)
    o_ref[...] = x_ref[...] + _rms(m, g_ref[...])


def _outproj(o_r, o_b, w_out, x, g, tm):
    m, d = x.shape
    ret_w, moba_w = o_r.shape[1], o_b.shape[1]
    tm = _tile(m, tm, 8)
    return pl.pallas_call(
        functools.partial(_outproj_kernel, ret_w=ret_w),
        out_shape=jax.ShapeDtypeStruct((m, d), F32),
        grid=(m // tm,),
        in_specs=[pl.BlockSpec((tm, ret_w), lambda i: (i, 0)),
                  pl.BlockSpec((tm, moba_w), lambda i: (i, 0)),
                  pl.BlockSpec((ret_w + moba_w, d), lambda i: (0, 0)),
                  pl.BlockSpec((tm, d), lambda i: (i, 0)),
                  pl.BlockSpec((1, d), lambda i: (0, 0))],
        out_specs=pl.BlockSpec((tm, d), lambda i: (i, 0)),
        compiler_params=_params("parallel"),
        name="outproj",
    )(o_r, o_b, w_out, x, g.reshape(1, d))


def _retention_kernel(lg_ref, q_ref, k_ref, v_ref, g_ref, s0_ref, o_ref, sfin_ref, s_ref, decay_ref,
                      *, chunk, mxu_dtype):
    h = pl.program_id(1)
    c = pl.program_id(2)
    lg = lg_ref[h]

    @pl.when(c == 0)
    def _():
        s_ref[...] = s0_ref[...]
        ii = lax.broadcasted_iota(jnp.int32, (chunk, chunk), 0)
        jj = lax.broadcasted_iota(jnp.int32, (chunk, chunk), 1)
        diff = (ii - jj).astype(F32)
        decay_ref[...] = jnp.where(diff >= 0, jnp.exp(lg * jnp.maximum(diff, 0.0)), 0.0)

    q = q_ref[...]
    k = k_ref[...] * (HEAD_DIM ** -0.5)
    v = v_ref[...]
    qm, km, vm = q.astype(mxu_dtype), k.astype(mxu_dtype), v.astype(mxu_dtype)
    idx = lax.broadcasted_iota(jnp.int32, (chunk, 1), 0).astype(F32)
    s = s_ref[...]

    inner = lax.dot_general(qm, km, _NT, preferred_element_type=F32) * decay_ref[...]
    o = jnp.dot(inner.astype(mxu_dtype), vm, preferred_element_type=F32)
    cross = jnp.exp(lg * (idx + 1.0))
    o = o + jnp.dot(qm, s.astype(mxu_dtype), preferred_element_type=F32) * cross

    wk = jnp.exp(lg * (chunk - 1.0 - idx))
    kw = (k * wk).astype(mxu_dtype)
    s_decay = jnp.exp(jnp.full((1, HEAD_DIM), lg * chunk, F32))
    s_new = s_decay * s + lax.dot_general(kw, vm, _TN, preferred_element_type=F32)
    s_ref[...] = s_new

    mu = jnp.mean(o, axis=-1, keepdims=True)
    oc = o - mu
    var = jnp.mean(oc * oc, axis=-1, keepdims=True)
    gate = g_ref[...]
    silu = gate / (1.0 + jnp.exp(-gate))
    o_ref[...] = (oc * lax.rsqrt(var + EPS) * silu).astype(o_ref.dtype)

    @pl.when(c == pl.num_programs(2) - 1)
    def _():
        sfin_ref[...] = s_new


def _retention(z3, s0, lg, n_heads, chunk, mxu_dtype, out_dtype):
    b, t, _ = z3.shape
    chunk = _tile(t, chunk, 8)

    def col(group):
        return pl.BlockSpec((None, chunk, HEAD_DIM), lambda bi, h, c: (bi, c, group * n_heads + h))

    state_spec = pl.BlockSpec((None, None, HEAD_DIM, HEAD_DIM), lambda bi, h, c: (bi, h, 0, 0))
    return pl.pallas_call(
        functools.partial(_retention_kernel, chunk=chunk, mxu_dtype=mxu_dtype),
        out_shape=(jax.ShapeDtypeStruct((b, t, n_heads * HEAD_DIM), out_dtype),
                   jax.ShapeDtypeStruct((b, n_heads, HEAD_DIM, HEAD_DIM), F32)),
        grid=(b, n_heads, t // chunk),
        in_specs=[pl.BlockSpec(memory_space=pltpu.SMEM), col(0), col(1), col(2), col(3), state_spec],
        out_specs=(pl.BlockSpec((None, chunk, HEAD_DIM), lambda bi, h, c: (bi, c, h)), state_spec),
        scratch_shapes=[pltpu.VMEM((HEAD_DIM, HEAD_DIM), F32), pltpu.VMEM((chunk, chunk), F32)],
        compiler_params=_params("parallel", "parallel", "arbitrary"),
        name="retention",
    )(lg, z3, z3, z3, z3, s0)


def _topk_rank(bs, n_cols, n_past):
    col = lax.broadcasted_iota(jnp.int32, bs.shape, 1)
    cnt = jnp.zeros(bs.shape, jnp.int32)
    for n in range(n_cols):
        sc = bs[:, n:n + 1]
        beats = jnp.where(sc > bs, 1, jnp.where((sc == bs) & (n < col), 1, 0))
        cnt = cnt + jnp.where(n < n_past, beats, 0)
    return cnt, col


def _moba_prompt_kernel(slopes_ref, q_ref, k_ref, v_ref, o_ref, kb_ref, vb_ref, kmean_ref, bias_ref,
                        *, nb, nbp):
    blk = MOBA_BLOCK
    h = pl.program_id(1)
    i = pl.program_id(2)
    slope = slopes_ref[h]

    @pl.when(i == 0)
    def _():
        kmean_ref[...] = jnp.zeros_like(kmean_ref)
        for n in range(nb):
            kblk = k_ref[n * blk:(n + 1) * blk, :]
            kmean_ref[n:n + 1, :] = jnp.mean(kblk, axis=0, keepdims=True)
            kb_ref[n * blk:(n + 1) * blk, :] = kblk.astype(BF16)
            vb_ref[n * blk:(n + 1) * blk, :] = v_ref[n * blk:(n + 1) * blk, :].astype(BF16)

    qf = q_ref[...]
    bs = lax.dot_general(qf, kmean_ref[...], _NT, precision=lax.Precision.HIGHEST,
                         preferred_element_type=F32)
    cnt, col = _topk_rank(bs, nbp, i)
    bias = jnp.where((col < i) & (cnt < MOBA_TOPK), 0.0, NEG)
    for n in range(nb):
        bias_ref[n] = jnp.broadcast_to(bias[:, n:n + 1], (blk, HEAD_DIM))

    qb = (qf * (HEAD_DIM ** -0.5)).astype(BF16)
    rr = lax.broadcasted_iota(jnp.int32, (blk, blk), 0)
    cc = lax.broadcasted_iota(jnp.int32, (blk, blk), 1)
    crow = lax.broadcasted_iota(jnp.int32, (1, blk), 1)

    k0 = pl.multiple_of(i * blk, blk)
    s = lax.dot_general(qb, kb_ref[pl.ds(k0, blk), :], _NT, preferred_element_type=F32)
    s = jnp.where(cc <= rr, s + slope * crow.astype(F32), NEG)
    m0 = jnp.max(s, axis=1, keepdims=True)
    p = jnp.exp(s - m0)
    l0 = jnp.sum(p, axis=1, keepdims=True)
    acc0 = jnp.dot(p.astype(BF16), vb_ref[pl.ds(k0, blk), :], preferred_element_type=F32)

    def body(j, carry):
        m, l, acc = carry
        kj0 = pl.multiple_of(j * blk, blk)
        sj = lax.dot_general(qb, kb_ref[pl.ds(kj0, blk), :], _NT, preferred_element_type=F32)
        arow = slope * (crow - (i - j) * blk).astype(F32)
        bj = bias_ref[j]
        sj = sj + arow + jnp.concatenate([bj, bj], axis=1)
        m_new = jnp.maximum(m, jnp.max(sj, axis=1, keepdims=True))
        alpha = jnp.exp(m - m_new)
        pj = jnp.exp(sj - m_new)
        l_new = alpha * l + jnp.sum(pj, axis=1, keepdims=True)
        acc_new = alpha * acc + jnp.dot(pj.astype(BF16), vb_ref[pl.ds(kj0, blk), :],
                                        preferred_element_type=F32)
        return m_new, l_new, acc_new

    _, l, acc = lax.fori_loop(0, i, body, (m0, l0, acc0))
    o_ref[...] = (acc / l).astype(o_ref.dtype)


def _moba_prompt(z3, slopes, n_heads, q_group, k_group, v_group):
    b, t, _ = z3.shape
    blk = MOBA_BLOCK
    nb = t // blk
    nbp = -(-nb // 8) * 8

    def col(group, rows, whole):
        if whole:
            return pl.BlockSpec((None, rows, HEAD_DIM), lambda bi, h, i: (bi, 0, group * n_heads + h))
        return pl.BlockSpec((None, rows, HEAD_DIM), lambda bi, h, i: (bi, i, group * n_heads + h))

    return pl.pallas_call(
        functools.partial(_moba_prompt_kernel, nb=nb, nbp=nbp),
        out_shape=jax.ShapeDtypeStruct((b, t, n_heads * HEAD_DIM), BF16),
        grid=(b, n_heads, nb),
        in_specs=[pl.BlockSpec(memory_space=pltpu.SMEM),
                  col(q_group, blk, False), col(k_group, t, True), col(v_group, t, True)],
        out_specs=pl.BlockSpec((None, blk, HEAD_DIM), lambda bi, h, i: (bi, i, h)),
        scratch_shapes=[pltpu.VMEM((t, HEAD_DIM), BF16), pltpu.VMEM((t, HEAD_DIM), BF16),
                        pltpu.VMEM((nbp, HEAD_DIM), F32), pltpu.VMEM((nb, blk, HEAD_DIM), F32)],
        compiler_params=_params("parallel", "parallel", "arbitrary"),
        name="moba_prompt",
    )(slopes, z3, z3, z3)


PAGES_PER_STEP = 8


def _pagesum_kernel(pt_ref, *refs):
    del pt_ref
    o_ref = refs[-1]
    for r, page_ref in enumerate(refs[:-1]):
        o_ref[r:r + 1, :] = jnp.sum(page_ref[...], axis=0, keepdims=True)


def _moba_pagesum(cache2, pt_flat, n_seq, n_pages):
    _, page, w = cache2.shape
    pps = _tile(n_pages, PAGES_PER_STEP, 1)

    def page_spec(r):
        return pl.BlockSpec((None, page, w), lambda b, s, pt: (pt[b * n_pages + s * pps + r], 0, 0))

    return pl.pallas_call(
        _pagesum_kernel,
        out_shape=jax.ShapeDtypeStruct((n_seq, n_pages, w), F32),
        grid_spec=pltpu.PrefetchScalarGridSpec(
            num_scalar_prefetch=1,
            grid=(n_seq, n_pages // pps),
            in_specs=[page_spec(r) for r in range(pps)],
            out_specs=pl.BlockSpec((None, pps, w), lambda b, s, pt: (b, s, 0))),
        compiler_params=_params("parallel", "arbitrary"),
        name="moba_pagesum",
    )(pt_flat, *([cache2] * pps))


def _moba_select_kernel(ps_ref, q_ref, o_ref, *, n_heads, nb, w):
    ps = ps_ref[...]
    ppb = ps.shape[1] // w
    ksum = ps[:, 0:w]
    for r in range(1, ppb):
        ksum = ksum + ps[:, r * w:(r + 1) * w]
    kmean = ksum * (1.0 / MOBA_BLOCK)
    q = q_ref[...]
    lane = lax.broadcasted_iota(jnp.int32, (q.shape[0], HEAD_DIM), 1)
    for h in range(n_heads):
        sl = slice(h * HEAD_DIM, (h + 1) * HEAD_DIM)
        bs = lax.dot_general(q[:, sl], kmean[:, sl], _NT, precision=lax.Precision.HIGHEST,
                             preferred_element_type=F32)
        cnt, col = _topk_rank(bs, nb, nb)
        out = jnp.zeros(lane.shape, jnp.int32)
        for r in range(MOBA_TOPK):
            idx = jnp.sum(jnp.where(cnt == r, col, 0).astype(F32), axis=1, keepdims=True)
            out = jnp.where(lane == r, idx.astype(jnp.int32), out)
        o_ref[h] = out


def _moba_select(psum, z3, n_heads, q_group, page):
    b, n_pages, w = psum.shape
    tq = z3.shape[1]
    ppb = MOBA_BLOCK // page
    nb = n_pages // ppb
    assert n_pages % ppb == 0 and nb >= MOBA_TOPK and tq <= MOBA_BLOCK
    ps2 = psum.reshape(b, nb, ppb * w)
    return pl.pallas_call(
        functools.partial(_moba_select_kernel, n_heads=n_heads, nb=nb, w=w),
        out_shape=jax.ShapeDtypeStruct((b, n_heads, tq, HEAD_DIM), jnp.int32),
        grid=(b,),
        in_specs=[pl.BlockSpec((None, nb, ppb * w), lambda bi: (bi, 0, 0)),
                  pl.BlockSpec((None, tq, w), lambda bi: (bi, 0, q_group))],
        out_specs=pl.BlockSpec((None, n_heads, tq, HEAD_DIM), lambda bi: (bi, 0, 0, 0)),
        compiler_params=_params("parallel"),
        name="moba_select",
    )(ps2, z3)


def _moba_sample_kernel(pt_ref, sel_ref, slopes_ref, q_ref, kn_ref, vn_ref, *refs,
                        n_heads, n_sel_pages, page, past_len, ppb):
    del pt_ref
    k_refs = refs[:n_sel_pages]
    v_refs = refs[n_sel_pages:2 * n_sel_pages]
    o_ref, kpad_ref, vpad_ref, tmp_ref = refs[2 * n_sel_pages:]
    b = pl.program_id(0)
    h = pl.program_id(1)
    t = pl.program_id(2)
    tq = q_ref.shape[0]
    slope = slopes_ref[h]

    kpad_ref[...] = jnp.zeros_like(kpad_ref)
    vpad_ref[...] = jnp.zeros_like(vpad_ref)
    kpad_ref[0:tq, :] = kn_ref[...]
    vpad_ref[0:tq, :] = vn_ref[...]

    q = q_ref[...] * (HEAD_DIM ** -0.5)
    row = lax.broadcasted_iota(jnp.int32, (tq, page), 0)
    c = lax.broadcasted_iota(jnp.int32, (tq, page), 1)
    base = ((b * n_heads + h) * tq + t) * MOBA_TOPK

    scores = []
    for n in range(n_sel_pages):
        kpos0 = sel_ref[base + n // ppb] * MOBA_BLOCK + (n % ppb) * page
        s = lax.dot_general(q, k_refs[n][...], _NT, preferred_element_type=F32)
        dist = (past_len + row - kpos0 - c).astype(F32)
        scores.append(s - slope * dist)
    s_own = lax.dot_general(q, kpad_ref[...], _NT, preferred_element_type=F32)
    s_own = jnp.where(c <= row, s_own - slope * (row - c).astype(F32), NEG)
    scores.append(s_own)

    m = scores[0].max(axis=1, keepdims=True)
    for s in scores[1:]:
        m = jnp.maximum(m, s.max(axis=1, keepdims=True))
    l = jnp.zeros((tq, 1), F32)
    acc = jnp.zeros((tq, HEAD_DIM), F32)
    for n, s in enumerate(scores):
        p = jnp.exp(s - m)
        l = l + jnp.sum(p, axis=1, keepdims=True)
        vals = v_refs[n][...] if n < n_sel_pages else vpad_ref[...]
        acc = acc + jnp.dot(p, vals, preferred_element_type=F32)
    tmp_ref[...] = acc / l
    o_ref[pl.ds(t, 1), :] = tmp_ref[pl.ds(t, 1), :]


def _moba_sample(z3, cache_k2, cache_v2, pt_flat, sel_flat, slopes, n_heads, n_pages,
                 q_group, k_group, v_group):
    b, tq, _ = z3.shape
    _, page, _ = cache_k2.shape
    ppb = MOBA_BLOCK // page
    n_sel_pages = MOBA_TOPK * ppb

    def col(group):
        return pl.BlockSpec((None, tq, HEAD_DIM), lambda bi, h, t, pt, sel: (bi, 0, group * n_heads + h))

    def page_spec(n):
        def index(bi, h, t, pt, sel):
            blk = sel[((bi * n_heads + h) * tq + t) * MOBA_TOPK + n // ppb]
            return (pt[bi * n_pages + blk * ppb + n % ppb], 0, h)
        return pl.BlockSpec((None, page, HEAD_DIM), index)

    pages = [page_spec(n) for n in range(n_sel_pages)]
    return pl.pallas_call(
        functools.partial(_moba_sample_kernel, n_heads=n_heads, n_sel_pages=n_sel_pages, page=page,
                          past_len=n_pages * page, ppb=ppb),
        out_shape=jax.ShapeDtypeStruct((b, tq, n_heads * HEAD_DIM), F32),
        grid_spec=pltpu.PrefetchScalarGridSpec(
            num_scalar_prefetch=2,
            grid=(b, n_heads, tq),
            in_specs=[pl.BlockSpec(memory_space=pltpu.SMEM), col(q_group), col(k_group), col(v_group)]
                     + pages + pages,
            out_specs=pl.BlockSpec((None, tq, HEAD_DIM), lambda bi, h, t, pt, sel: (bi, 0, h)),
            scratch_shapes=[pltpu.VMEM((page, HEAD_DIM), F32), pltpu.VMEM((page, HEAD_DIM), F32),
                            pltpu.VMEM((tq, HEAD_DIM), F32)]),
        compiler_params=_params("parallel", "parallel", "arbitrary"),
        name="moba_sample",
    )(pt_flat, sel_flat, slopes, z3, z3, z3, *([cache_k2] * n_sel_pages), *([cache_v2] * n_sel_pages))


def _pool_kernel(x_ref, halo_ref, g0_ref, g1_ref, w_ref, sc_ref, y_ref, st_ref, ext_ref,
                 *, tm, group_w, halo_is_state, mxu_dtype):
    i = pl.program_id(1)
    x = x_ref[...]
    h = _rms(x, g0_ref[...])
    if halo_is_state:
        ext_ref[0:POOL_HALO - POOL_STATE, :] = jnp.zeros((POOL_HALO - POOL_STATE, x.shape[1]), F32)
        ext_ref[POOL_HALO - POOL_STATE:POOL_HALO, :] = halo_ref[...]
    else:
        ext_ref[0:POOL_HALO, :] = jnp.where(i > 0, _rms(halo_ref[...], g0_ref[...]), 0.0)
    ext_ref[POOL_HALO:POOL_HALO + tm, :] = h

    t = i * tm + lax.broadcasted_iota(jnp.int32, (tm, 1), 0)
    ys = []
    for gi, win in enumerate(POOL_WINDOWS):
        cols = slice(gi * group_w, (gi + 1) * group_w)
        acc = ext_ref[POOL_HALO:POOL_HALO + tm, cols]
        for s in range(1, win):
            acc = acc + ext_ref[POOL_HALO - s:POOL_HALO - s + tm, cols]
        if halo_is_state:
            mean = acc * (1.0 / win)
        else:
            mean = acc / jnp.minimum(t + 1, win).astype(F32)
        pooled = mean - h[:, cols]
        ys.append(jnp.dot(pooled.astype(mxu_dtype), w_ref[gi].astype(mxu_dtype),
                          preferred_element_type=F32))
    y = jnp.concatenate(ys, axis=1) * sc_ref[...]
    y_ref[...] = x + _rms(y, g1_ref[...])
    st_ref[...] = ext_ref[POOL_HALO + tm - POOL_STATE:POOL_HALO + tm, :]


def _pool(x3, state, g0, g1, w_pool, scale, tm, mxu_dtype):
    b, t, d = x3.shape
    tm = _tile(t, tm, POOL_HALO)
    group_w = d // len(POOL_WINDOWS)
    halo_is_state = state is not None
    if halo_is_state:
        assert t == tm
        halo, halo_spec = state, pl.BlockSpec((None, POOL_STATE, d), lambda bi, i: (bi, 0, 0))
    else:
        per = tm // POOL_HALO
        halo = x3
        halo_spec = pl.BlockSpec((None, POOL_HALO, d), lambda bi, i: (bi, jnp.maximum(i * per - 1, 0), 0))
    vec = pl.BlockSpec((1, d), lambda bi, i: (0, 0))
    return pl.pallas_call(
        functools.partial(_pool_kernel, tm=tm, group_w=group_w, halo_is_state=halo_is_state,
                          mxu_dtype=mxu_dtype),
        out_shape=(jax.ShapeDtypeStruct((b, t, d), F32), jax.ShapeDtypeStruct((b, POOL_STATE, d), F32)),
        grid=(b, t // tm),
        in_specs=[pl.BlockSpec((None, tm, d), lambda bi, i: (bi, i, 0)), halo_spec, vec, vec,
                  pl.BlockSpec(w_pool.shape, lambda bi, i: (0, 0, 0)), vec],
        out_specs=(pl.BlockSpec((None, tm, d), lambda bi, i: (bi, i, 0)),
                   pl.BlockSpec((None, POOL_STATE, d), lambda bi, i: (bi, 0, 0))),
        scratch_shapes=[pltpu.VMEM((POOL_HALO + tm, d), F32)],
        compiler_params=_params("parallel", "arbitrary"),
        name="pool",
    )(x3, halo, g0.reshape(1, d), g1.reshape(1, d), w_pool, scale.reshape(1, d))


def _trunk(x3, ret0, pool0, paged, weights, lg, slopes, *, row_tile, small):
    w_in, w_out, w_pool_bf, w_pool, pool_scale, w_up, w_down, norm_g = weights
    b, t, d = x3.shape
    depth = w_up.shape[0]
    n_ret = ret0.shape[2]
    x = x3.reshape(b * t, d)
    mxu_small = F32 if small else BF16
    new_k, new_v, new_ret, new_pool = [], [], [], []
    for l in range(depth):
        if l % 2 == 0:
            e = l // 2
            n_moba = (w_in.shape[2] // HEAD_DIM - 4 * n_ret) // 3
            assert n_moba == n_ret
            z = _norm_matmul(x, norm_g[l, 0], w_in[e], row_tile, 512)
            z3 = z.reshape(b, t, -1)
            o_r, s_new = _retention(z3, ret0[e], lg, n_ret, 256, mxu_small, F32 if small else BF16)
            k_off = (4 * n_ret + n_moba) * HEAD_DIM
            v_off = (4 * n_ret + 2 * n_moba) * HEAD_DIM
            new_k.append(z3[:, :, k_off:v_off].reshape(b, t, n_moba, HEAD_DIM))
            new_v.append(z3[:, :, v_off:].reshape(b, t, n_moba, HEAD_DIM))
            if paged is None:
                o_b = _moba_prompt(z3, slopes, n_moba, 4, 5, 6)
            else:
                cache_k, cache_v, page_table = paged
                n_seq, n_pages = page_table.shape
                pool_pages, page = cache_k.shape[1], cache_k.shape[2]
                ck2 = cache_k[e].reshape(pool_pages, page, n_moba * HEAD_DIM)
                cv2 = cache_v[e].reshape(pool_pages, page, n_moba * HEAD_DIM)
                pt_flat = page_table.reshape(-1)
                psum = _moba_pagesum(ck2, pt_flat, n_seq, n_pages)
                sel = _moba_select(psum, z3, n_moba, 4, page)
                sel_flat = sel[..., :MOBA_TOPK].reshape(-1)
                o_b = _moba_sample(z3, ck2, cv2, pt_flat, sel_flat, slopes, n_moba, n_pages, 4, 5, 6)
            x = _outproj(o_r.reshape(b * t, -1), o_b.reshape(b * t, -1), w_out[e], x, norm_g[l, 1],
                         row_tile)
            new_ret.append(s_new)
        else:
            o = l // 2
            wp = w_pool[o] if small else w_pool_bf[o]
            y3, p_new = _pool(x.reshape(b, t, d), None if pool0 is None else pool0[o],
                              norm_g[l, 0], norm_g[l, 1], wp, pool_scale[o], 256, mxu_small)
            x = y3.reshape(b * t, d)
            new_pool.append(p_new)
        x = _mlp(x, norm_g[l, 2], norm_g[l, 3], w_up[l], w_down[l], row_tile, 512)
    return (x.reshape(b, t, d), jnp.stack(new_k), jnp.stack(new_v), jnp.stack(new_ret),
            jnp.stack(new_pool))


def kernel(x_prompt, x_sample, cache_k, cache_v, state_ret, state_pool, page_table, w_in, w_out, w_pool,
           pool_scale, w_up, w_down, norm_g):
    n_ret = state_ret.shape[2]
    n_moba = cache_k.shape[3]
    lg = jnp.log1p(-jnp.exp2(-5.0 - jnp.arange(n_ret, dtype=F32)))
    slopes = jnp.exp2(-8.0 * (jnp.arange(n_moba, dtype=F32) + 1.0) / n_moba)
    weights = (w_in.astype(BF16), w_out.astype(BF16), w_pool.astype(BF16), w_pool, pool_scale,
               w_up.astype(BF16), w_down.astype(BF16), norm_g)
    n_even = state_ret.shape[0]
    ret_zero = jnp.zeros((n_even, x_prompt.shape[0], n_ret, HEAD_DIM, HEAD_DIM), F32)

    y_p, k_p, v_p, ret_p, pool_p = _trunk(x_prompt, ret_zero, None, None, weights, lg, slopes,
                                          row_tile=512, small=False)
    y_s, k_s, v_s, ret_s, pool_s = _trunk(x_sample, state_ret, state_pool,
                                          (cache_k, cache_v, page_table), weights, lg, slopes,
                                          row_tile=512, small=True)
    return (y_p, y_s, k_p, v_p, k_s, v_s, ret_p, ret_s, pool_p, pool_s)
```

```python
import functools

import jax
import jax.numpy as jnp
from jax import lax
from jax.experimental import pallas as pl
from jax.experimental.pallas import tpu as pltpu

F32 = jnp.float32
BF16 = jnp.bfloat16

HEAD_DIM = 128
MOBA_BLOCK = 256
MOBA_TOPK = 3
POOL_WINDOWS = (2, 4, 8, 16)
POOL_HALO = 16
POOL_STATE = max(POOL_WINDOWS) - 1
EPS = 1e-6
NEG = -1e30
VMEM_LIMIT = 48 * 1024 * 1024

_NT = (((1,), (1,)), ((), ()))
_TN = (((0,), (0,)), ((), ()))


def _params(*sem):
    return pltpu.CompilerParams(dimension_semantics=sem, vmem_limit_bytes=VMEM_LIMIT)


def _tile(n, target, align):
    if n <= target:
        return n
    t = target - target % align
    while n % t:
        t -= align
    return t


def _rms(x, g):
    return x * lax.rsqrt(jnp.mean(x * x, axis=-1, keepdims=True) + EPS) * g


def _inproj_kernel(x_ref, g_ref, w_ref, z_ref, k_ref, v_ref, h_ref, *, n_main, n_kv):
    j = pl.program_id(1)

    @pl.when(j == 0)
    def _():
        h_ref[...] = _rms(x_ref[...], g_ref[...]).astype(BF16)

    def emit(o_ref):
        o_ref[...] = jnp.dot(h_ref[...], w_ref[...], preferred_element_type=F32)

    pl.when(j < n_main)(lambda: emit(z_ref))
    pl.when((j >= n_main) & (j < n_main + n_kv))(lambda: emit(k_ref))
    pl.when(j >= n_main + n_kv)(lambda: emit(v_ref))


def _inproj(x, g, w, layer, group_w, tm, tn):
    m, d = x.shape
    tm, tn = _tile(m, tm, 8), _tile(group_w, tn, 128)
    n_main, n_kv = 5 * group_w // tn, group_w // tn
    assert w.shape[2] == 7 * group_w
    return pl.pallas_call(
        functools.partial(_inproj_kernel, n_main=n_main, n_kv=n_kv),
        out_shape=(jax.ShapeDtypeStruct((m, 5 * group_w), F32),
                   jax.ShapeDtypeStruct((m, group_w), F32),
                   jax.ShapeDtypeStruct((m, group_w), F32)),
        grid=(m // tm, n_main + 2 * n_kv),
        in_specs=[pl.BlockSpec((tm, d), lambda i, j: (i, 0)),
                  pl.BlockSpec((1, d), lambda i, j: (0, 0)),
                  pl.BlockSpec((None, d, tn), lambda i, j: (layer, 0, j))],
        out_specs=(pl.BlockSpec((tm, tn), lambda i, j: (i, jnp.minimum(j, n_main - 1))),
                   pl.BlockSpec((tm, tn), lambda i, j: (i, jnp.clip(j - n_main, 0, n_kv - 1))),
                   pl.BlockSpec((tm, tn), lambda i, j: (i, jnp.clip(j - n_main - n_kv, 0, n_kv - 1)))),
        scratch_shapes=[pltpu.VMEM((tm, d), BF16)],
        compiler_params=_params("parallel", "arbitrary"),
        name="in_proj",
    )(x, g.reshape(1, d), w)


def _mlp_kernel(x_ref, g2_ref, g3_ref, wu_ref, wd_ref, o_ref, h_ref, acc_ref):
    f = pl.program_id(1)

    @pl.when(f == 0)
    def _():
        h_ref[...] = _rms(x_ref[...], g2_ref[...]).astype(BF16)
        acc_ref[...] = jnp.zeros_like(acc_ref)

    u = jnp.dot(h_ref[...], wu_ref[...], preferred_element_type=F32)
    a = jnp.square(jnp.maximum(u, 0.0)).astype(BF16)
    acc_ref[...] += jnp.dot(a, wd_ref[...], preferred_element_type=F32)

    @pl.when(f == pl.num_programs(1) - 1)
    def _():
        o_ref[...] = x_ref[...] + _rms(acc_ref[...], g3_ref[...])


def _mlp(x, g2, g3, w_up, w_down, layer, tm, tf):
    m, d = x.shape
    ff = w_up.shape[2]
    tm, tf = _tile(m, tm, 8), _tile(ff, tf, 128)
    return pl.pallas_call(
        _mlp_kernel,
        out_shape=jax.ShapeDtypeStruct((m, d), F32),
        grid=(m // tm, ff // tf),
        in_specs=[pl.BlockSpec((tm, d), lambda i, f: (i, 0)),
                  pl.BlockSpec((1, d), lambda i, f: (0, 0)),
                  pl.BlockSpec((1, d), lambda i, f: (0, 0)),
                  pl.BlockSpec((None, d, tf), lambda i, f: (layer, 0, f)),
                  pl.BlockSpec((None, tf, d), lambda i, f: (layer, f, 0))],
        out_specs=pl.BlockSpec((tm, d), lambda i, f: (i, 0)),
        scratch_shapes=[pltpu.VMEM((tm, d), BF16), pltpu.VMEM((tm, d), F32)],
        compiler_params=_params("parallel", "arbitrary"),
        name="mlp",
    )(x, g2.reshape(1, d), g3.reshape(1, d), w_up, w_down)


def _outproj_kernel(or_ref, ob_ref, w_ref, x_ref, g_ref, o_ref, *, ret_w):
    m = jnp.dot(or_ref[...].astype(BF16), w_ref[0:ret_w, :], preferred_element_type=F32)
    m = m + jnp.dot(ob_ref[...].astype(BF16), w_ref[ret_w:, :], preferred_element_type=F32)
    o_ref[...] = x_ref[...] + _rms(m, g_ref[...])


def _outproj(o_r, o_b, w_out, layer, x, g, tm):
    m, d = x.shape
    ret_w, moba_w = o_r.shape[1], o_b.shape[1]
    tm = _tile(m, tm, 8)
    return pl.pallas_call(
        functools.partial(_outproj_kernel, ret_w=ret_w),
        out_shape=jax.ShapeDtypeStruct((m, d), F32),
        grid=(m // tm,),
        in_specs=[pl.BlockSpec((tm, ret_w), lambda i: (i, 0)),
                  pl.BlockSpec((tm, moba_w), lambda i: (i, 0)),
                  pl.BlockSpec((None, ret_w + moba_w, d), lambda i: (layer, 0, 0)),
                  pl.BlockSpec((tm, d), lambda i: (i, 0)),
                  pl.BlockSpec((1, d), lambda i: (0, 0))],
        out_specs=pl.BlockSpec((tm, d), lambda i: (i, 0)),
        compiler_params=_params("parallel"),
        name="outproj",
    )(o_r, o_b, w_out, x, g.reshape(1, d))


def _retention_kernel(lg_ref, q_ref, k_ref, v_ref, g_ref, s0_ref, o_ref, sfin_ref, s_ref, decay_ref,
                      *, chunk, mxu_dtype):
    h = pl.program_id(1)
    c = pl.program_id(2)
    lg = lg_ref[h]

    @pl.when(c == 0)
    def _():
        s_ref[...] = s0_ref[...]
        ii = lax.broadcasted_iota(jnp.int32, (chunk, chunk), 0)
        jj = lax.broadcasted_iota(jnp.int32, (chunk, chunk), 1)
        diff = (ii - jj).astype(F32)
        decay_ref[...] = jnp.where(diff >= 0, jnp.exp(lg * jnp.maximum(diff, 0.0)), 0.0)

    q = q_ref[...]
    k = k_ref[...] * (HEAD_DIM ** -0.5)
    v = v_ref[...]
    qm, km, vm = q.astype(mxu_dtype), k.astype(mxu_dtype), v.astype(mxu_dtype)
    idx = lax.broadcasted_iota(jnp.int32, (chunk, 1), 0).astype(F32)
    s = s_ref[...]

    inner = lax.dot_general(qm, km, _NT, preferred_element_type=F32) * decay_ref[...]
    o = jnp.dot(inner.astype(mxu_dtype), vm, preferred_element_type=F32)
    cross = jnp.exp(lg * (idx + 1.0))
    o = o + jnp.dot(qm, s.astype(mxu_dtype), preferred_element_type=F32) * cross

    wk = jnp.exp(lg * (chunk - 1.0 - idx))
    kw = (k * wk).astype(mxu_dtype)
    s_decay = jnp.exp(jnp.full((1, HEAD_DIM), lg * chunk, F32))
    s_new = s_decay * s + lax.dot_general(kw, vm, _TN, preferred_element_type=F32)
    s_ref[...] = s_new

    mu = jnp.mean(o, axis=-1, keepdims=True)
    oc = o - mu
    var = jnp.mean(oc * oc, axis=-1, keepdims=True)
    gate = g_ref[...]
    silu = gate / (1.0 + jnp.exp(-gate))
    o_ref[...] = (oc * lax.rsqrt(var + EPS) * silu).astype(o_ref.dtype)

    @pl.when(c == pl.num_programs(2) - 1)
    def _():
        sfin_ref[...] = s_new


def _retention(z3, s0, lg, n_heads, chunk, mxu_dtype, out_dtype):
    b, t, _ = z3.shape
    chunk = _tile(t, chunk, 8)

    def col(group):
        return pl.BlockSpec((None, chunk, HEAD_DIM), lambda bi, h, c: (bi, c, group * n_heads + h))

    state_spec = pl.BlockSpec((None, None, HEAD_DIM, HEAD_DIM), lambda bi, h, c: (bi, h, 0, 0))
    return pl.pallas_call(
        functools.partial(_retention_kernel, chunk=chunk, mxu_dtype=mxu_dtype),
        out_shape=(jax.ShapeDtypeStruct((b, t, n_heads * HEAD_DIM), out_dtype),
                   jax.ShapeDtypeStruct((b, n_heads, HEAD_DIM, HEAD_DIM), F32)),
        grid=(b, n_heads, t // chunk),
        in_specs=[pl.BlockSpec(memory_space=pltpu.SMEM), col(0), col(1), col(2), col(3), state_spec],
        out_specs=(pl.BlockSpec((None, chunk, HEAD_DIM), lambda bi, h, c: (bi, c, h)), state_spec),
        scratch_shapes=[pltpu.VMEM((HEAD_DIM, HEAD_DIM), F32), pltpu.VMEM((chunk, chunk), F32)],
        compiler_params=_params("parallel", "parallel", "arbitrary"),
        name="retention",
    )(lg, z3, z3, z3, z3, s0)


def _topk_rank(bs, n_blocks, n_past, axis):
    idx = lax.broadcasted_iota(jnp.int32, bs.shape, axis)
    cnt = jnp.zeros(bs.shape, jnp.int32)
    for n in range(n_blocks):
        sc = bs[n:n + 1, :] if axis == 0 else bs[:, n:n + 1]
        beats = jnp.where(sc > bs, 1, jnp.where((sc == bs) & (n < idx), 1, 0))
        cnt = cnt + jnp.where(n < n_past, beats, 0)
    return cnt, idx


MOBA_HEADS_PER_STEP = 4


def _moba_prompt_kernel(slopes_ref, q_ref, k_ref, v_ref, o_ref, kb_ref, vt_ref, kmean_ref, bias_ref,
                        alibi_ref, *, nb, nbp, hg):
    blk, d = MOBA_BLOCK, HEAD_DIM
    g = pl.program_id(1)
    i = pl.program_id(2)
    heads = range(hg)

    @pl.when(i == 0)
    def _():
        kmean_ref[...] = jnp.zeros_like(kmean_ref)
        for u in heads:
            for n in range(nb):
                kblk = k_ref[n * blk:(n + 1) * blk, u * d:(u + 1) * d]
                kmean_ref[u, n:n + 1, :] = jnp.mean(kblk, axis=0, keepdims=True)
                kb_ref[u, n] = kblk.astype(BF16)
                vt_ref[u, n] = v_ref[n * blk:(n + 1) * blk, u * d:(u + 1) * d].T.astype(BF16)

    kk = lax.broadcasted_iota(jnp.int32, (blk, blk), 0)
    qq = lax.broadcasted_iota(jnp.int32, (blk, blk), 1)
    slopes = [slopes_ref[g * hg + u] for u in heads]
    qtbs, carry0 = [], []
    for u in heads:
        qt = q_ref[:, u * d:(u + 1) * d].T
        bs = jnp.dot(kmean_ref[u], qt, precision=lax.Precision.HIGHEST,
                     preferred_element_type=F32)
        cnt, bidx = _topk_rank(bs, nbp, i, 0)
        bias_ref[u] = jnp.where((bidx < i) & (cnt < MOBA_TOPK), 0.0, NEG)
        qtb = (qt * (d ** -0.5)).astype(BF16)
        alibi = slopes[u] * kk.astype(F32)
        alibi_ref[u] = alibi
        s = jnp.dot(kb_ref[u, i], qtb, preferred_element_type=F32)
        s = jnp.where(kk <= qq, s + alibi, NEG)
        m0 = jnp.max(s, axis=0, keepdims=True)
        p = jnp.exp(s - m0)
        l0 = jnp.sum(p, axis=0, keepdims=True)
        acc0 = jnp.dot(vt_ref[u, i], p.astype(BF16), preferred_element_type=F32)
        qtbs.append(qtb)
        carry0 += [m0, l0, acc0]

    def body(j, carry):
        shift = ((i - j) * blk).astype(F32)
        raw = [jnp.dot(kb_ref[u, j], qtbs[u], preferred_element_type=F32) for u in heads]
        stats, probs = [], []
        for u in heads:
            m, l, _ = carry[3 * u:3 * u + 3]
            brow = bias_ref[u, pl.ds(j, 1), :] - slopes[u] * shift
            sj = raw[u] + alibi_ref[u] + brow
            m_new = jnp.maximum(m, jnp.max(sj, axis=0, keepdims=True))
            alpha = jnp.exp(m - m_new)
            pj = jnp.exp(sj - m_new)
            stats.append((m_new, alpha, alpha * l + jnp.sum(pj, axis=0, keepdims=True)))
            probs.append(pj.astype(BF16))
        out = []
        for u in heads:
            m_new, alpha, l_new = stats[u]
            pv = jnp.dot(vt_ref[u, j], probs[u], preferred_element_type=F32)
            out += [m_new, l_new, alpha * carry[3 * u + 2] + pv]
        return tuple(out)

    fin = lax.fori_loop(0, i, body, tuple(carry0))
    for u in heads:
        _, l, acc = fin[3 * u:3 * u + 3]
        o_ref[:, u * d:(u + 1) * d] = (acc / l).T.astype(o_ref.dtype)


def _moba_prompt(z3, k3, v3, slopes, n_heads, q_group):
    b, t, _ = z3.shape
    blk, d = MOBA_BLOCK, HEAD_DIM
    nb = t // blk
    nbp = -(-nb // 8) * 8
    hg = MOBA_HEADS_PER_STEP if n_heads % MOBA_HEADS_PER_STEP == 0 else 1
    n_groups = n_heads // hg
    whole = pl.BlockSpec((None, t, hg * d), lambda bi, g, i: (bi, 0, g))
    return pl.pallas_call(
        functools.partial(_moba_prompt_kernel, nb=nb, nbp=nbp, hg=hg),
        out_shape=jax.ShapeDtypeStruct((b, t, n_heads * d), BF16),
        grid=(b, n_groups, nb),
        in_specs=[pl.BlockSpec(memory_space=pltpu.SMEM),
                  pl.BlockSpec((None, blk, hg * d), lambda bi, g, i: (bi, i, q_group * n_groups + g)),
                  whole, whole],
        out_specs=pl.BlockSpec((None, blk, hg * d), lambda bi, g, i: (bi, i, g)),
        scratch_shapes=[pltpu.VMEM((hg, nb, blk, d), BF16), pltpu.VMEM((hg, nb, d, blk), BF16),
                        pltpu.VMEM((hg, nbp, d), F32), pltpu.VMEM((hg, nbp, blk), F32),
                        pltpu.VMEM((hg, blk, blk), F32)],
        compiler_params=_params("parallel", "parallel", "arbitrary"),
        name="moba_prompt",
    )(slopes, z3, k3, v3)


PAGES_PER_STEP = 16


def _blocksum_kernel(pt_ref, *refs, ppb, n_heads):
    del pt_ref
    o_ref = refs[-1]
    pages = refs[:-1]
    for blk in range(len(pages) // ppb):
        s = jnp.sum(pages[blk * ppb][...], axis=0)
        for r in range(1, ppb):
            s = s + jnp.sum(pages[blk * ppb + r][...], axis=0)
        for hh in range(n_heads):
            o_ref[hh, blk:blk + 1, :] = s[hh:hh + 1, :]


def _moba_blocksum(cache, layer, pt_flat, n_seq, n_pages):
    _, _, page, n_heads, d = cache.shape
    ppb = MOBA_BLOCK // page
    pps = _tile(n_pages, PAGES_PER_STEP, 8 * ppb)
    assert n_pages % ppb == 0 and pps % ppb == 0

    def page_spec(r):
        return pl.BlockSpec((None, None, page, n_heads, d),
                            lambda b, s, pt: (layer, pt[b * n_pages + s * pps + r], 0, 0, 0))

    return pl.pallas_call(
        functools.partial(_blocksum_kernel, ppb=ppb, n_heads=n_heads),
        out_shape=jax.ShapeDtypeStruct((n_seq, n_heads, n_pages // ppb, d), F32),
        grid_spec=pltpu.PrefetchScalarGridSpec(
            num_scalar_prefetch=1,
            grid=(n_seq, n_pages // pps),
            in_specs=[page_spec(r) for r in range(pps)],
            out_specs=pl.BlockSpec((None, n_heads, pps // ppb, d), lambda b, s, pt: (b, 0, s, 0))),
        compiler_params=_params("parallel", "arbitrary"),
        name="moba_blocksum",
    )(pt_flat, *([cache] * pps))


def _moba_select_kernel(ks_ref, q_ref, o_ref, *, n_heads, nb):
    q = q_ref[...]
    lane = lax.broadcasted_iota(jnp.int32, (q.shape[0], HEAD_DIM), 1)
    for h in range(n_heads):
        kmean = ks_ref[h] * (1.0 / MOBA_BLOCK)
        bs = lax.dot_general(q[:, h * HEAD_DIM:(h + 1) * HEAD_DIM], kmean, _NT,
                             precision=lax.Precision.HIGHEST, preferred_element_type=F32)
        cnt, col = _topk_rank(bs, nb, nb, 1)
        out = jnp.zeros(lane.shape, jnp.int32)
        for r in range(MOBA_TOPK):
            idx = jnp.sum(jnp.where(cnt == r, col, 0).astype(F32), axis=1, keepdims=True)
            out = jnp.where(lane == r, idx.astype(jnp.int32), out)
        o_ref[h] = out


def _moba_select(ksum, z3, q_group):
    b, n_heads, nb, d = ksum.shape
    tq = z3.shape[1]
    assert nb >= MOBA_TOPK and tq <= MOBA_BLOCK
    return pl.pallas_call(
        functools.partial(_moba_select_kernel, n_heads=n_heads, nb=nb),
        out_shape=jax.ShapeDtypeStruct((b, n_heads, tq, HEAD_DIM), jnp.int32),
        grid=(b,),
        in_specs=[pl.BlockSpec((None, n_heads, nb, d), lambda bi: (bi, 0, 0, 0)),
                  pl.BlockSpec((None, tq, n_heads * d), lambda bi: (bi, 0, q_group))],
        out_specs=pl.BlockSpec((None, n_heads, tq, HEAD_DIM), lambda bi: (bi, 0, 0, 0)),
        compiler_params=_params("parallel"),
        name="moba_select",
    )(ksum, z3)


def _moba_sample_kernel(pt_ref, sel_ref, slopes_ref, q_ref, kn_ref, vn_ref, ck_ref, cv_ref, o_ref,
                        kbuf, vbuf, sem, *, layer, n_heads, n_pages, page, ppb, tq):
    b = pl.program_id(0)
    h = pl.program_id(1)
    step = b * n_heads + h
    n_steps = pl.num_programs(0) * n_heads
    npg = MOBA_TOPK * ppb
    slot = step % 2

    def copies(bi, hi, to_slot):
        out = []
        for t in range(tq):
            for n in range(npg):
                blk = sel_ref[((bi * n_heads + hi) * tq + t) * MOBA_TOPK + n // ppb]
                pg = pt_ref[bi * n_pages + blk * ppb + n % ppb]
                out.append(pltpu.make_async_copy(ck_ref.at[layer, pg, :, hi, :],
                                                 kbuf.at[to_slot, t * npg + n], sem.at[to_slot, 0]))
                out.append(pltpu.make_async_copy(cv_ref.at[layer, pg, :, hi, :],
                                                 vbuf.at[to_slot, t * npg + n], sem.at[to_slot, 1]))
        return out

    @pl.when(step == 0)
    def _():
        for cp in copies(b, h, slot):
            cp.start()

    @pl.when(step + 1 < n_steps)
    def _():
        nxt = step + 1
        for cp in copies(nxt // n_heads, nxt % n_heads, 1 - slot):
            cp.start()

    for cp in copies(b, h, slot):
        cp.wait()

    slope = slopes_ref[h]
    past_len = n_pages * page
    rowp = lax.broadcasted_iota(jnp.int32, (page, 1), 0)
    rowq = lax.broadcasted_iota(jnp.int32, (tq, 1), 0)
    kn = kn_ref[...]
    vn = vn_ref[...]

    def body(t, carry):
        q_t = q_ref[pl.ds(t, 1), :] * (HEAD_DIM ** -0.5)
        base = ((b * n_heads + h) * tq + t) * MOBA_TOPK
        scores = []
        for n in range(npg):
            kpos0 = sel_ref[base + n // ppb] * MOBA_BLOCK + (n % ppb) * page
            s = jnp.sum(kbuf[slot, t * npg + n] * q_t, axis=1, keepdims=True)
            dist = (past_len + t - kpos0 - rowp).astype(F32)
            scores.append(s - slope * dist)
        s_own = jnp.sum(kn * q_t, axis=1, keepdims=True)
        s_own = jnp.where(rowq <= t, s_own - slope * (t - rowq).astype(F32), NEG)
        m = jnp.max(s_own, axis=0, keepdims=True)
        for s in scores:
            m = jnp.maximum(m, jnp.max(s, axis=0, keepdims=True))
        p_own = jnp.exp(s_own - m)
        l = jnp.sum(p_own, axis=0, keepdims=True)
        acc = jnp.sum(p_own * vn, axis=0, keepdims=True)
        for n, s in enumerate(scores):
            p = jnp.exp(s - m)
            l = l + jnp.sum(p, axis=0, keepdims=True)
            acc = acc + jnp.sum(p * vbuf[slot, t * npg + n], axis=0, keepdims=True)
        o_ref[pl.ds(t, 1), :] = acc / l
        return carry

    lax.fori_loop(0, tq, body, 0)


def _moba_sample(z3, k3, v3, cache_k, cache_v, layer, pt_flat, sel_flat, slopes, n_pages, q_group):
    b, tq, _ = z3.shape
    _, _, page, n_heads, d = cache_k.shape
    ppb = MOBA_BLOCK // page
    n_bufs = tq * MOBA_TOPK * ppb
    new_rows = pl.BlockSpec((None, tq, d), lambda bi, h, pt, sel: (bi, 0, h))
    return pl.pallas_call(
        functools.partial(_moba_sample_kernel, layer=layer, n_heads=n_heads, n_pages=n_pages,
                          page=page, ppb=ppb, tq=tq),
        out_shape=jax.ShapeDtypeStruct((b, tq, n_heads * d), F32),
        grid_spec=pltpu.PrefetchScalarGridSpec(
            num_scalar_prefetch=2,
            grid=(b, n_heads),
            in_specs=[pl.BlockSpec(memory_space=pltpu.SMEM),
                      pl.BlockSpec((None, tq, d), lambda bi, h, pt, sel: (bi, 0, q_group * n_heads + h)),
                      new_rows, new_rows,
                      pl.BlockSpec(memory_space=pl.ANY), pl.BlockSpec(memory_space=pl.ANY)],
            out_specs=pl.BlockSpec((None, tq, d), lambda bi, h, pt, sel: (bi, 0, h)),
            scratch_shapes=[pltpu.VMEM((2, n_bufs, page, d), F32), pltpu.VMEM((2, n_bufs, page, d), F32),
                            pltpu.SemaphoreType.DMA((2, 2))]),
        compiler_params=_params("arbitrary", "arbitrary"),
        name="moba_sample",
    )(pt_flat, sel_flat, slopes, z3, k3, v3, cache_k, cache_v)


def _pool_kernel(x_ref, halo_ref, g0_ref, g1_ref, w_ref, sc_ref, y_ref, st_ref, ext_ref,
                 *, tm, group_w, halo_is_state, mxu_dtype):
    i = pl.program_id(1)
    x = x_ref[...]
    h = _rms(x, g0_ref[...])
    if halo_is_state:
        ext_ref[0:POOL_HALO - POOL_STATE, :] = jnp.zeros((POOL_HALO - POOL_STATE, x.shape[1]), F32)
        ext_ref[POOL_HALO - POOL_STATE:POOL_HALO, :] = halo_ref[...]
    else:
        ext_ref[0:POOL_HALO, :] = jnp.where(i > 0, _rms(halo_ref[...], g0_ref[...]), 0.0)
    ext_ref[POOL_HALO:POOL_HALO + tm, :] = h

    t = i * tm + lax.broadcasted_iota(jnp.int32, (tm, 1), 0)
    ys = []
    for gi, win in enumerate(POOL_WINDOWS):
        cols = slice(gi * group_w, (gi + 1) * group_w)
        acc = ext_ref[POOL_HALO:POOL_HALO + tm, cols]
        for s in range(1, win):
            acc = acc + ext_ref[POOL_HALO - s:POOL_HALO - s + tm, cols]
        if halo_is_state:
            mean = acc * (1.0 / win)
        else:
            mean = acc / jnp.minimum(t + 1, win).astype(F32)
        pooled = mean - h[:, cols]
        ys.append(jnp.dot(pooled.astype(mxu_dtype), w_ref[gi].astype(mxu_dtype),
                          preferred_element_type=F32))
    y = jnp.concatenate(ys, axis=1) * sc_ref[...]
    y_ref[...] = x + _rms(y, g1_ref[...])
    st_ref[...] = ext_ref[POOL_HALO + tm - POOL_STATE:POOL_HALO + tm, :]


def _pool(x3, state, g0, g1, w_pool, layer, scale, tm, mxu_dtype):
    b, t, d = x3.shape
    tm = _tile(t, tm, POOL_HALO)
    group_w = d // len(POOL_WINDOWS)
    halo_is_state = state is not None
    if halo_is_state:
        assert t == tm
        halo, halo_spec = state, pl.BlockSpec((None, POOL_STATE, d), lambda bi, i: (bi, 0, 0))
    else:
        per = tm // POOL_HALO
        halo = x3
        halo_spec = pl.BlockSpec((None, POOL_HALO, d), lambda bi, i: (bi, jnp.maximum(i * per - 1, 0), 0))
    vec = pl.BlockSpec((1, d), lambda bi, i: (0, 0))
    return pl.pallas_call(
        functools.partial(_pool_kernel, tm=tm, group_w=group_w, halo_is_state=halo_is_state,
                          mxu_dtype=mxu_dtype),
        out_shape=(jax.ShapeDtypeStruct((b, t, d), F32), jax.ShapeDtypeStruct((b, POOL_STATE, d), F32)),
        grid=(b, t // tm),
        in_specs=[pl.BlockSpec((None, tm, d), lambda bi, i: (bi, i, 0)), halo_spec, vec, vec,
                  pl.BlockSpec((None,) + w_pool.shape[1:], lambda bi, i: (layer, 0, 0, 0)), vec],
        out_specs=(pl.BlockSpec((None, tm, d), lambda bi, i: (bi, i, 0)),
                   pl.BlockSpec((None, POOL_STATE, d), lambda bi, i: (bi, 0, 0))),
        scratch_shapes=[pltpu.VMEM((POOL_HALO + tm, d), F32)],
        compiler_params=_params("parallel", "arbitrary"),
        name="pool",
    )(x3, halo, g0.reshape(1, d), g1.reshape(1, d), w_pool, scale.reshape(1, d))


def _trunk(x3, ret0, pool0, paged, weights, lg, slopes, *, row_tile, small):
    w_in, w_out, w_pool_bf, w_pool, pool_scale, w_up, w_down, norm_g = weights
    b, t, d = x3.shape
    depth = w_up.shape[0]
    n_ret = ret0.shape[2]
    group_w = n_ret * HEAD_DIM
    x = x3.reshape(b * t, d)
    mxu_small = F32 if small else BF16
    new_k, new_v, new_ret, new_pool = [], [], [], []
    for l in range(depth):
        if l % 2 == 0:
            e = l // 2
            z, k_new, v_new = _inproj(x, norm_g[l, 0], w_in, e, group_w, 2 * row_tile, 512)
            z3 = z.reshape(b, t, -1)
            k3 = k_new.reshape(b, t, group_w)
            v3 = v_new.reshape(b, t, group_w)
            o_r, s_new = _retention(z3, ret0[e], lg, n_ret, 256, mxu_small, F32 if small else BF16)
            new_k.append(k3.reshape(b, t, n_ret, HEAD_DIM))
            new_v.append(v3.reshape(b, t, n_ret, HEAD_DIM))
            if paged is None:
                o_b = _moba_prompt(z3, k3, v3, slopes, n_ret, 4)
            else:
                cache_k, cache_v, page_table = paged
                n_seq, n_pages = page_table.shape
                pt_flat = page_table.reshape(-1)
                ksum = _moba_blocksum(cache_k, e, pt_flat, n_seq, n_pages)
                sel = _moba_select(ksum, z3, 4)
                sel_flat = sel[..., :MOBA_TOPK].reshape(-1)
                o_b = _moba_sample(z3, k3, v3, cache_k, cache_v, e, pt_flat, sel_flat, slopes, n_pages, 4)
            x = _outproj(o_r.reshape(b * t, -1), o_b.reshape(b * t, -1), w_out, e, x, norm_g[l, 1],
                         row_tile)
            new_ret.append(s_new)
        else:
            o = l // 2
            wp = w_pool if small else w_pool_bf
            y3, p_new = _pool(x.reshape(b, t, d), None if pool0 is None else pool0[o],
                              norm_g[l, 0], norm_g[l, 1], wp, o, pool_scale[o], 256, mxu_small)
            x = y3.reshape(b * t, d)
            new_pool.append(p_new)
        x = _mlp(x, norm_g[l, 2], norm_g[l, 3], w_up, w_down, l, row_tile, 512)
    return (x.reshape(b, t, d), jnp.stack(new_k), jnp.stack(new_v), jnp.stack(new_ret),
            jnp.stack(new_pool))


def kernel(x_prompt, x_sample, cache_k, cache_v, state_ret, state_pool, page_table, w_in, w_out, w_pool,
           pool_scale, w_up, w_down, norm_g):
    n_ret = state_ret.shape[2]
    n_moba = cache_k.shape[3]
    assert n_moba == n_ret and w_in.shape[2] == 7 * n_ret * HEAD_DIM
    lg = jnp.log1p(-jnp.exp2(-5.0 - jnp.arange(n_ret, dtype=F32)))
    slopes = jnp.exp2(-8.0 * (jnp.arange(n_moba, dtype=F32) + 1.0) / n_moba)
    weights = (w_in.astype(BF16), w_out.astype(BF16), w_pool.astype(BF16), w_pool, pool_scale,
               w_up.astype(BF16), w_down.astype(BF16), norm_g)
    n_even = state_ret.shape[0]
    ret_zero = jnp.zeros((n_even, x_prompt.shape[0], n_ret, HEAD_DIM, HEAD_DIM), F32)

    y_p, k_p, v_p, ret_p, pool_p = _trunk(x_prompt, ret_zero, None, None, weights, lg, slopes,
                                          row_tile=512, small=False)
    y_s, k_s, v_s, ret_s, pool_s = _trunk(x_sample, state_ret, state_pool,
                                          (cache_k, cache_v, page_table), weights, lg, slopes,
                                          row_tile=512, small=True)
    return (y_p, y_s, k_p, v_p, k_s, v_s, ret_p, ret_s, pool_p, pool_s)
```

```python
import functools

import jax
import jax.numpy as jnp
from jax import lax
from jax.experimental import pallas as pl
from jax.experimental.pallas import tpu as pltpu

F32 = jnp.float32
BF16 = jnp.bfloat16

HEAD_DIM = 128
MOBA_BLOCK = 256
MOBA_TOPK = 3
POOL_WINDOWS = (2, 4, 8, 16)
POOL_HALO = 16
POOL_STATE = max(POOL_WINDOWS) - 1
EPS = 1e-6
NEG = -1e30
LOG2E = 1.4426950408889634
VMEM_LIMIT = 48 * 1024 * 1024
VMEM_LIMIT_BIG = 56 * 1024 * 1024

_NT = (((1,), (1,)), ((), ()))
_TN = (((0,), (0,)), ((), ()))


def _params(*sem, vmem=VMEM_LIMIT):
    return pltpu.CompilerParams(dimension_semantics=sem, vmem_limit_bytes=vmem)


def _tile(n, target, align):
    if n <= target:
        return n
    t = target - target % align
    while n % t:
        t -= align
    return t


def _rms(x, g):
    return x * lax.rsqrt(jnp.mean(x * x, axis=-1, keepdims=True) + EPS) * g


def _inproj_kernel(x_ref, g_ref, w_ref, z_ref, k_ref, v_ref, h_ref, *, n_main, n_kv):
    j = pl.program_id(1)

    @pl.when(j == 0)
    def _():
        h_ref[...] = _rms(x_ref[...], g_ref[...]).astype(BF16)

    def emit(o_ref):
        o_ref[...] = jnp.dot(h_ref[...], w_ref[...], preferred_element_type=F32)

    pl.when(j < n_main)(lambda: emit(z_ref))
    pl.when((j >= n_main) & (j < n_main + n_kv))(lambda: emit(k_ref))
    pl.when(j >= n_main + n_kv)(lambda: emit(v_ref))


def _inproj(x, g, w, layer, group_w, tm, tn):
    m, d = x.shape
    tm, tn = _tile(m, tm, 8), _tile(group_w, tn, 128)
    n_main, n_kv = 5 * group_w // tn, group_w // tn
    assert w.shape[2] == 7 * group_w
    return pl.pallas_call(
        functools.partial(_inproj_kernel, n_main=n_main, n_kv=n_kv),
        out_shape=(jax.ShapeDtypeStruct((m, 5 * group_w), F32),
                   jax.ShapeDtypeStruct((m, group_w), F32),
                   jax.ShapeDtypeStruct((m, group_w), F32)),
        grid=(m // tm, n_main + 2 * n_kv),
        in_specs=[pl.BlockSpec((tm, d), lambda i, j: (i, 0)),
                  pl.BlockSpec((1, d), lambda i, j: (0, 0)),
                  pl.BlockSpec((None, d, tn), lambda i, j: (layer, 0, j))],
        out_specs=(pl.BlockSpec((tm, tn), lambda i, j: (i, jnp.minimum(j, n_main - 1))),
                   pl.BlockSpec((tm, tn), lambda i, j: (i, jnp.clip(j - n_main, 0, n_kv - 1))),
                   pl.BlockSpec((tm, tn), lambda i, j: (i, jnp.clip(j - n_main - n_kv, 0, n_kv - 1)))),
        scratch_shapes=[pltpu.VMEM((tm, d), BF16)],
        compiler_params=_params("parallel", "arbitrary"),
        name="in_proj",
    )(x, g.reshape(1, d), w)


def _mlp_kernel(x_ref, g2_ref, g3_ref, wu_ref, wd_ref, o_ref, h_ref):
    f = pl.program_id(1)

    @pl.when(f == 0)
    def _():
        h_ref[...] = _rms(x_ref[...], g2_ref[...]).astype(BF16)
        o_ref[...] = jnp.zeros_like(o_ref)

    u = jnp.dot(h_ref[...], wu_ref[...], preferred_element_type=F32)
    a = jnp.square(jnp.maximum(u, 0.0)).astype(BF16)
    o_ref[...] += jnp.dot(a, wd_ref[...], preferred_element_type=F32)

    @pl.when(f == pl.num_programs(1) - 1)
    def _():
        o_ref[...] = x_ref[...] + _rms(o_ref[...], g3_ref[...])


def _mlp(x, g2, g3, w_up, w_down, layer, tm, tf):
    m, d = x.shape
    ff = w_up.shape[2]
    tm, tf = _tile(m, tm, 8), _tile(ff, tf, 128)
    return pl.pallas_call(
        _mlp_kernel,
        out_shape=jax.ShapeDtypeStruct((m, d), F32),
        grid=(m // tm, ff // tf),
        in_specs=[pl.BlockSpec((tm, d), lambda i, f: (i, 0)),
                  pl.BlockSpec((1, d), lambda i, f: (0, 0)),
                  pl.BlockSpec((1, d), lambda i, f: (0, 0)),
                  pl.BlockSpec((None, d, tf), lambda i, f: (layer, 0, f)),
                  pl.BlockSpec((None, tf, d), lambda i, f: (layer, f, 0))],
        out_specs=pl.BlockSpec((tm, d), lambda i, f: (i, 0)),
        scratch_shapes=[pltpu.VMEM((tm, d), BF16)],
        compiler_params=_params("parallel", "arbitrary", vmem=VMEM_LIMIT_BIG),
        name="mlp",
    )(x, g2.reshape(1, d), g3.reshape(1, d), w_up, w_down)


def _outproj_kernel(or_ref, ob_ref, w_ref, x_ref, g_ref, o_ref, *, ret_w):
    m = jnp.dot(or_ref[...].astype(BF16), w_ref[0:ret_w, :], preferred_element_type=F32)
    m = m + jnp.dot(ob_ref[...].astype(BF16), w_ref[ret_w:, :], preferred_element_type=F32)
    o_ref[...] = x_ref[...] + _rms(m, g_ref[...])


def _outproj(o_r, o_b, w_out, layer, x, g, tm):
    m, d = x.shape
    ret_w, moba_w = o_r.shape[1], o_b.shape[1]
    tm = _tile(m, tm, 8)
    return pl.pallas_call(
        functools.partial(_outproj_kernel, ret_w=ret_w),
        out_shape=jax.ShapeDtypeStruct((m, d), F32),
        grid=(m // tm,),
        in_specs=[pl.BlockSpec((tm, ret_w), lambda i: (i, 0)),
                  pl.BlockSpec((tm, moba_w), lambda i: (i, 0)),
                  pl.BlockSpec((None, ret_w + moba_w, d), lambda i: (layer, 0, 0)),
                  pl.BlockSpec((tm, d), lambda i: (i, 0)),
                  pl.BlockSpec((1, d), lambda i: (0, 0))],
        out_specs=pl.BlockSpec((tm, d), lambda i: (i, 0)),
        compiler_params=_params("parallel"),
        name="outproj",
    )(o_r, o_b, w_out, x, g.reshape(1, d))


RET_HEADS_PER_STEP = 4


def _retention_kernel(lg_ref, q_ref, k_ref, v_ref, g_ref, s0_ref, o_ref, sfin_ref, s_ref, decay_ref,
                      *, chunk, hg, mxu_dtype):
    d = HEAD_DIM
    g = pl.program_id(1)
    c = pl.program_id(2)
    heads = range(hg)
    lgs = [lg_ref[g * hg + u] for u in heads]

    @pl.when(c == 0)
    def _():
        s_ref[...] = s0_ref[...]
        ii = lax.broadcasted_iota(jnp.int32, (chunk, chunk), 0)
        jj = lax.broadcasted_iota(jnp.int32, (chunk, chunk), 1)
        diff = (ii - jj).astype(F32)
        for u in heads:
            decay_ref[u] = jnp.where(diff >= 0, jnp.exp(lgs[u] * jnp.maximum(diff, 0.0)), 0.0)

    idx = lax.broadcasted_iota(jnp.int32, (chunk, 1), 0).astype(F32)
    cols = [slice(u * d, (u + 1) * d) for u in heads]
    ks = [k_ref[:, cols[u]] * (d ** -0.5) for u in heads]
    qm = [q_ref[:, cols[u]].astype(mxu_dtype) for u in heads]
    vm = [v_ref[:, cols[u]].astype(mxu_dtype) for u in heads]
    states = [s_ref[u] for u in heads]

    raw = [lax.dot_general(qm[u], ks[u].astype(mxu_dtype), _NT, preferred_element_type=F32)
           for u in heads]
    qs = [jnp.dot(qm[u], states[u].astype(mxu_dtype), preferred_element_type=F32) for u in heads]
    inner = [(raw[u] * decay_ref[u]).astype(mxu_dtype) for u in heads]
    kw = [(ks[u] * jnp.exp(lgs[u] * (chunk - 1.0 - idx))).astype(mxu_dtype) for u in heads]
    intra = [jnp.dot(inner[u], vm[u], preferred_element_type=F32) for u in heads]
    kv = [lax.dot_general(kw[u], vm[u], _TN, preferred_element_type=F32) for u in heads]

    last = c == pl.num_programs(2) - 1
    for u in heads:
        o = intra[u] + qs[u] * jnp.exp(lgs[u] * (idx + 1.0))
        s_new = jnp.exp(jnp.full((1, d), lgs[u] * chunk, F32)) * states[u] + kv[u]
        s_ref[u] = s_new
        mu = jnp.mean(o, axis=-1, keepdims=True)
        oc = o - mu
        var = jnp.mean(oc * oc, axis=-1, keepdims=True)
        gate = g_ref[:, cols[u]]
        silu = gate / (1.0 + jnp.exp(-gate))
        o_ref[:, cols[u]] = (oc * lax.rsqrt(var + EPS) * silu).astype(o_ref.dtype)

        @pl.when(last)
        def _():
            sfin_ref[u] = s_new


def _retention(z3, s0, lg, n_heads, chunk, mxu_dtype, out_dtype):
    b, t, _ = z3.shape
    d = HEAD_DIM
    chunk = _tile(t, chunk, 8)
    hg = _tile(n_heads, RET_HEADS_PER_STEP, 1)
    n_groups = n_heads // hg

    def col(group):
        return pl.BlockSpec((None, chunk, hg * d), lambda bi, g, c: (bi, c, group * n_groups + g))

    state_spec = pl.BlockSpec((None, hg, d, d), lambda bi, g, c: (bi, g, 0, 0))
    return pl.pallas_call(
        functools.partial(_retention_kernel, chunk=chunk, hg=hg, mxu_dtype=mxu_dtype),
        out_shape=(jax.ShapeDtypeStruct((b, t, n_heads * d), out_dtype),
                   jax.ShapeDtypeStruct((b, n_heads, d, d), F32)),
        grid=(b, n_groups, t // chunk),
        in_specs=[pl.BlockSpec(memory_space=pltpu.SMEM), col(0), col(1), col(2), col(3), state_spec],
        out_specs=(pl.BlockSpec((None, chunk, hg * d), lambda bi, g, c: (bi, c, g)), state_spec),
        scratch_shapes=[pltpu.VMEM((hg, d, d), F32), pltpu.VMEM((hg, chunk, chunk), F32)],
        compiler_params=_params("parallel", "parallel", "arbitrary"),
        name="retention",
    )(lg, z3, z3, z3, z3, s0)


def _topk_rank(bs, n_blocks, n_past, axis):
    idx = lax.broadcasted_iota(jnp.int32, bs.shape, axis)
    cnt = jnp.zeros(bs.shape, jnp.int32)
    for n in range(n_blocks):
        sc = bs[n:n + 1, :] if axis == 0 else bs[:, n:n + 1]
        beats = jnp.where(sc > bs, 1, jnp.where((sc == bs) & (n < idx), 1, 0))
        cnt = cnt + jnp.where(n < n_past, beats, 0)
    return cnt, idx


MOBA_HEADS_PER_STEP = 8


def _moba_prompt_kernel(slopes_ref, q_ref, k_ref, v_ref, o_ref, vt_ref, kmean_ref, bias_ref,
                        alibi_ref, *, nb, nbp, hg):
    blk, d = MOBA_BLOCK, HEAD_DIM
    g = pl.program_id(1)
    i = pl.program_id(2)
    heads = range(hg)

    @pl.when(i == 0)
    def _():
        kmean_ref[...] = jnp.zeros_like(kmean_ref)
        for u in heads:
            for n in range(nb):
                kblk = k_ref[n * blk:(n + 1) * blk, u * d:(u + 1) * d]
                kmean_ref[u, n:n + 1, :] = jnp.mean(kblk, axis=0, keepdims=True)
                vt_ref[u, n] = v_ref[n * blk:(n + 1) * blk, u * d:(u + 1) * d].T.astype(BF16)

    def kblock(u, n):
        return k_ref[pl.ds(pl.multiple_of(n * blk, blk), blk), u * d:(u + 1) * d].astype(BF16)

    kk = lax.broadcasted_iota(jnp.int32, (blk, blk), 0)
    qq = lax.broadcasted_iota(jnp.int32, (blk, blk), 1)
    slopes = [slopes_ref[g * hg + u] * LOG2E for u in heads]
    qtbs, carry0 = [], []
    for u in heads:
        qt = q_ref[:, u * d:(u + 1) * d].T
        bs = jnp.dot(kmean_ref[u], qt, precision=lax.Precision.HIGHEST,
                     preferred_element_type=F32)
        cnt, bidx = _topk_rank(bs, nbp, i, 0)
        bias_ref[u] = jnp.where((bidx < i) & (cnt < MOBA_TOPK), 0.0, NEG)
        qtb = (qt * (d ** -0.5 * LOG2E)).astype(BF16)
        alibi = slopes[u] * kk.astype(F32)
        alibi_ref[u] = alibi
        s = jnp.dot(kblock(u, i), qtb, preferred_element_type=F32)
        s = jnp.where(kk <= qq, s + alibi, NEG)
        m0 = jnp.max(s, axis=0, keepdims=True)
        p = jnp.exp2(s - m0)
        l0 = jnp.sum(p, axis=0, keepdims=True)
        acc0 = jnp.dot(vt_ref[u, i], p.astype(BF16), preferred_element_type=F32)
        qtbs.append(qtb)
        carry0 += [m0, l0, acc0]

    def body(j, carry):
        shift = ((i - j) * blk).astype(F32)
        raw = [jnp.dot(kblock(u, j), qtbs[u], preferred_element_type=F32) for u in heads]
        stats, probs = [], []
        for u in heads:
            m, l, _ = carry[3 * u:3 * u + 3]
            brow = bias_ref[u, pl.ds(j, 1), :] - slopes[u] * shift
            sj = raw[u] + alibi_ref[u] + brow
            m_new = jnp.maximum(m, jnp.max(sj, axis=0, keepdims=True))
            alpha = jnp.exp2(m - m_new)
            pj = jnp.exp2(sj - m_new)
            stats.append((m_new, alpha, alpha * l + jnp.sum(pj, axis=0, keepdims=True)))
            probs.append(pj.astype(BF16))
        out = []
        for u in heads:
            m_new, alpha, l_new = stats[u]
            pv = jnp.dot(vt_ref[u, j], probs[u], preferred_element_type=F32)
            out += [m_new, l_new, alpha * carry[3 * u + 2] + pv]
        return tuple(out)

    fin = lax.fori_loop(0, i, body, tuple(carry0))
    for u in heads:
        _, l, acc = fin[3 * u:3 * u + 3]
        o_ref[:, u * d:(u + 1) * d] = (acc / l).T.astype(o_ref.dtype)


def _moba_prompt(z3, k3, v3, slopes, n_heads, q_group):
    b, t, _ = z3.shape
    blk, d = MOBA_BLOCK, HEAD_DIM
    nb = t // blk
    nbp = -(-nb // 8) * 8
    hg = _tile(n_heads, MOBA_HEADS_PER_STEP, 1)
    n_groups = n_heads // hg
    whole = pl.BlockSpec((None, t, hg * d), lambda bi, g, i: (bi, 0, g), pipeline_mode=pl.Buffered(1))
    return pl.pallas_call(
        functools.partial(_moba_prompt_kernel, nb=nb, nbp=nbp, hg=hg),
        out_shape=jax.ShapeDtypeStruct((b, t, n_heads * d), BF16),
        grid=(b, n_groups, nb),
        in_specs=[pl.BlockSpec(memory_space=pltpu.SMEM),
                  pl.BlockSpec((None, blk, hg * d), lambda bi, g, i: (bi, i, q_group * n_groups + g)),
                  whole, whole],
        out_specs=pl.BlockSpec((None, blk, hg * d), lambda bi, g, i: (bi, i, g)),
        scratch_shapes=[pltpu.VMEM((hg, nb, d, blk), BF16),
                        pltpu.VMEM((hg, nbp, d), F32), pltpu.VMEM((hg, nbp, blk), F32),
                        pltpu.VMEM((hg, blk, blk), F32)],
        compiler_params=_params("parallel", "parallel", "arbitrary", vmem=VMEM_LIMIT_BIG),
        name="moba_prompt",
    )(slopes, z3, k3, v3)


PAGES_PER_STEP = 16


def _blocksum_kernel(pt_ref, *refs, ppb, n_heads):
    del pt_ref
    o_ref = refs[-1]
    pages = refs[:-1]
    for blk in range(len(pages) // ppb):
        s = jnp.sum(pages[blk * ppb][...], axis=0)
        for r in range(1, ppb):
            s = s + jnp.sum(pages[blk * ppb + r][...], axis=0)
        for hh in range(n_heads):
            o_ref[hh, blk:blk + 1, :] = s[hh:hh + 1, :]


def _moba_blocksum(cache, layer, pt_flat, n_seq, n_pages):
    _, _, page, n_heads, d = cache.shape
    ppb = MOBA_BLOCK // page
    pps = _tile(n_pages, PAGES_PER_STEP, 8 * ppb)
    assert n_pages % ppb == 0 and pps % ppb == 0

    def page_spec(r):
        return pl.BlockSpec((None, None, page, n_heads, d),
                            lambda b, s, pt: (layer, pt[b * n_pages + s * pps + r], 0, 0, 0))

    return pl.pallas_call(
        functools.partial(_blocksum_kernel, ppb=ppb, n_heads=n_heads),
        out_shape=jax.ShapeDtypeStruct((n_seq, n_heads, n_pages // ppb, d), F32),
        grid_spec=pltpu.PrefetchScalarGridSpec(
            num_scalar_prefetch=1,
            grid=(n_seq, n_pages // pps),
            in_specs=[page_spec(r) for r in range(pps)],
            out_specs=pl.BlockSpec((None, n_heads, pps // ppb, d), lambda b, s, pt: (b, 0, s, 0))),
        compiler_params=_params("parallel", "arbitrary"),
        name="moba_blocksum",
    )(pt_flat, *([cache] * pps))


def _moba_select_kernel(ks_ref, q_ref, o_ref, *, n_heads, nb):
    q = q_ref[...]
    lane = lax.broadcasted_iota(jnp.int32, (q.shape[0], HEAD_DIM), 1)
    for h in range(n_heads):
        kmean = ks_ref[h] * (1.0 / MOBA_BLOCK)
        bs = lax.dot_general(q[:, h * HEAD_DIM:(h + 1) * HEAD_DIM], kmean, _NT,
                             precision=lax.Precision.HIGHEST, preferred_element_type=F32)
        cnt, col = _topk_rank(bs, nb, nb, 1)
        out = jnp.zeros(lane.shape, jnp.int32)
        for r in range(MOBA_TOPK):
            idx = jnp.sum(jnp.where(cnt == r, col, 0).astype(F32), axis=1, keepdims=True)
            out = jnp.where(lane == r, idx.astype(jnp.int32), out)
        o_ref[h] = out


def _moba_select(ksum, z3, q_group):
    b, n_heads, nb, d = ksum.shape
    tq = z3.shape[1]
    assert nb >= MOBA_TOPK and tq <= MOBA_BLOCK
    return pl.pallas_call(
        functools.partial(_moba_select_kernel, n_heads=n_heads, nb=nb),
        out_shape=jax.ShapeDtypeStruct((b, n_heads, tq, HEAD_DIM), jnp.int32),
        grid=(b,),
        in_specs=[pl.BlockSpec((None, n_heads, nb, d), lambda bi: (bi, 0, 0, 0)),
                  pl.BlockSpec((None, tq, n_heads * d), lambda bi: (bi, 0, q_group))],
        out_specs=pl.BlockSpec((None, n_heads, tq, HEAD_DIM), lambda bi: (bi, 0, 0, 0)),
        compiler_params=_params("parallel"),
        name="moba_select",
    )(ksum, z3)


def _moba_sample_kernel(pt_ref, sel_ref, slopes_ref, q_ref, kn_ref, vn_ref, ck_ref, cv_ref, o_ref,
                        kbuf, vbuf, sem, *, layer, n_heads, n_pages, page, ppb, tq):
    b = pl.program_id(0)
    h = pl.program_id(1)
    step = b * n_heads + h
    n_steps = pl.num_programs(0) * n_heads
    npg = MOBA_TOPK * ppb
    slot = step % 2

    def copies(bi, hi, to_slot):
        out = []
        for t in range(tq):
            for n in range(npg):
                blk = sel_ref[((bi * n_heads + hi) * tq + t) * MOBA_TOPK + n // ppb]
                pg = pt_ref[bi * n_pages + blk * ppb + n % ppb]
                out.append(pltpu.make_async_copy(ck_ref.at[layer, pg, :, hi, :],
                                                 kbuf.at[to_slot, t * npg + n], sem.at[to_slot, 0]))
                out.append(pltpu.make_async_copy(cv_ref.at[layer, pg, :, hi, :],
                                                 vbuf.at[to_slot, t * npg + n], sem.at[to_slot, 1]))
        return out

    @pl.when(step == 0)
    def _():
        for cp in copies(b, h, slot):
            cp.start()

    @pl.when(step + 1 < n_steps)
    def _():
        nxt = step + 1
        for cp in copies(nxt // n_heads, nxt % n_heads, 1 - slot):
            cp.start()

    for cp in copies(b, h, slot):
        cp.wait()

    slope = slopes_ref[h]
    past_len = n_pages * page
    rowp = lax.broadcasted_iota(jnp.int32, (page, 1), 0)
    rowq = lax.broadcasted_iota(jnp.int32, (tq, 1), 0)
    kn = kn_ref[...]
    vn = vn_ref[...]

    def body(t, carry):
        q_t = q_ref[pl.ds(t, 1), :] * (HEAD_DIM ** -0.5)
        base = ((b * n_heads + h) * tq + t) * MOBA_TOPK
        scores = []
        for n in range(npg):
            kpos0 = sel_ref[base + n // ppb] * MOBA_BLOCK + (n % ppb) * page
            s = jnp.sum(kbuf[slot, t * npg + n] * q_t, axis=1, keepdims=True)
            dist = (past_len + t - kpos0 - rowp).astype(F32)
            scores.append(s - slope * dist)
        s_own = jnp.sum(kn * q_t, axis=1, keepdims=True)
        s_own = jnp.where(rowq <= t, s_own - slope * (t - rowq).astype(F32), NEG)
        m = jnp.max(s_own, axis=0, keepdims=True)
        for s in scores:
            m = jnp.maximum(m, jnp.max(s, axis=0, keepdims=True))
        p_own = jnp.exp(s_own - m)
        l = jnp.sum(p_own, axis=0, keepdims=True)
        acc = jnp.sum(p_own * vn, axis=0, keepdims=True)
        for n, s in enumerate(scores):
            p = jnp.exp(s - m)
            l = l + jnp.sum(p, axis=0, keepdims=True)
            acc = acc + jnp.sum(p * vbuf[slot, t * npg + n], axis=0, keepdims=True)
        o_ref[pl.ds(t, 1), :] = acc / l
        return carry

    lax.fori_loop(0, tq, body, 0)


def _moba_sample(z3, k3, v3, cache_k, cache_v, layer, pt_flat, sel_flat, slopes, n_pages, q_group):
    b, tq, _ = z3.shape
    _, _, page, n_heads, d = cache_k.shape
    ppb = MOBA_BLOCK // page
    n_bufs = tq * MOBA_TOPK * ppb
    new_rows = pl.BlockSpec((None, tq, d), lambda bi, h, pt, sel: (bi, 0, h))
    return pl.pallas_call(
        functools.partial(_moba_sample_kernel, layer=layer, n_heads=n_heads, n_pages=n_pages,
                          page=page, ppb=ppb, tq=tq),
        out_shape=jax.ShapeDtypeStruct((b, tq, n_heads * d), F32),
        grid_spec=pltpu.PrefetchScalarGridSpec(
            num_scalar_prefetch=2,
            grid=(b, n_heads),
            in_specs=[pl.BlockSpec(memory_space=pltpu.SMEM),
                      pl.BlockSpec((None, tq, d), lambda bi, h, pt, sel: (bi, 0, q_group * n_heads + h)),
                      new_rows, new_rows,
                      pl.BlockSpec(memory_space=pl.ANY), pl.BlockSpec(memory_space=pl.ANY)],
            out_specs=pl.BlockSpec((None, tq, d), lambda bi, h, pt, sel: (bi, 0, h)),
            scratch_shapes=[pltpu.VMEM((2, n_bufs, page, d), F32), pltpu.VMEM((2, n_bufs, page, d), F32),
                            pltpu.SemaphoreType.DMA((2, 2))]),
        compiler_params=_params("arbitrary", "arbitrary"),
        name="moba_sample",
    )(pt_flat, sel_flat, slopes, z3, k3, v3, cache_k, cache_v)


def _pool_kernel(x_ref, halo_ref, g0_ref, g1_ref, w_ref, sc_ref, y_ref, st_ref, ext_ref,
                 *, tm, group_w, halo_is_state, mxu_dtype):
    i = pl.program_id(1)
    x = x_ref[...]
    h = _rms(x, g0_ref[...])
    if halo_is_state:
        ext_ref[0:POOL_HALO - POOL_STATE, :] = jnp.zeros((POOL_HALO - POOL_STATE, x.shape[1]), F32)
        ext_ref[POOL_HALO - POOL_STATE:POOL_HALO, :] = halo_ref[...]
    else:
        ext_ref[0:POOL_HALO, :] = jnp.where(i > 0, _rms(halo_ref[...], g0_ref[...]), 0.0)
    ext_ref[POOL_HALO:POOL_HALO + tm, :] = h

    t = i * tm + lax.broadcasted_iota(jnp.int32, (tm, 1), 0)
    ys = []
    for gi, win in enumerate(POOL_WINDOWS):
        cols = slice(gi * group_w, (gi + 1) * group_w)
        acc = ext_ref[POOL_HALO:POOL_HALO + tm, cols]
        for s in range(1, win):
            acc = acc + ext_ref[POOL_HALO - s:POOL_HALO - s + tm, cols]
        if halo_is_state:
            mean = acc * (1.0 / win)
        else:
            mean = acc / jnp.minimum(t + 1, win).astype(F32)
        pooled = mean - h[:, cols]
        ys.append(jnp.dot(pooled.astype(mxu_dtype), w_ref[gi].astype(mxu_dtype),
                          preferred_element_type=F32))
    y = jnp.concatenate(ys, axis=1) * sc_ref[...]
    y_ref[...] = x + _rms(y, g1_ref[...])
    st_ref[...] = ext_ref[POOL_HALO + tm - POOL_STATE:POOL_HALO + tm, :]


def _pool(x3, state, g0, g1, w_pool, layer, scale, tm, mxu_dtype):
    b, t, d = x3.shape
    tm = _tile(t, tm, POOL_HALO)
    group_w = d // len(POOL_WINDOWS)
    halo_is_state = state is not None
    if halo_is_state:
        assert t == tm
        halo, halo_spec = state, pl.BlockSpec((None, POOL_STATE, d), lambda bi, i: (bi, 0, 0))
    else:
        per = tm // POOL_HALO
        halo = x3
        halo_spec = pl.BlockSpec((None, POOL_HALO, d), lambda bi, i: (bi, jnp.maximum(i * per - 1, 0), 0))
    vec = pl.BlockSpec((1, d), lambda bi, i: (0, 0))
    return pl.pallas_call(
        functools.partial(_pool_kernel, tm=tm, group_w=group_w, halo_is_state=halo_is_state,
                          mxu_dtype=mxu_dtype),
        out_shape=(jax.ShapeDtypeStruct((b, t, d), F32), jax.ShapeDtypeStruct((b, POOL_STATE, d), F32)),
        grid=(b, t // tm),
        in_specs=[pl.BlockSpec((None, tm, d), lambda bi, i: (bi, i, 0)), halo_spec, vec, vec,
                  pl.BlockSpec((None,) + w_pool.shape[1:], lambda bi, i: (layer, 0, 0, 0)), vec],
        out_specs=(pl.BlockSpec((None, tm, d), lambda bi, i: (bi, i, 0)),
                   pl.BlockSpec((None, POOL_STATE, d), lambda bi, i: (bi, 0, 0))),
        scratch_shapes=[pltpu.VMEM((POOL_HALO + tm, d), F32)],
        compiler_params=_params("parallel", "arbitrary"),
        name="pool",
    )(x3, halo, g0.reshape(1, d), g1.reshape(1, d), w_pool, scale.reshape(1, d))


def _trunk(x3, ret0, pool0, paged, weights, lg, slopes, *, row_tile, small):
    w_in, w_out, w_pool_bf, w_pool, pool_scale, w_up, w_down, norm_g = weights
    b, t, d = x3.shape
    depth = w_up.shape[0]
    n_ret = ret0.shape[2]
    group_w = n_ret * HEAD_DIM
    x = x3.reshape(b * t, d)
    mxu_small = F32 if small else BF16
    new_k, new_v, new_ret, new_pool = [], [], [], []
    for l in range(depth):
        if l % 2 == 0:
            e = l // 2
            z, k_new, v_new = _inproj(x, norm_g[l, 0], w_in, e, group_w, 2 * row_tile, 512)
            z3 = z.reshape(b, t, -1)
            k3 = k_new.reshape(b, t, group_w)
            v3 = v_new.reshape(b, t, group_w)
            o_r, s_new = _retention(z3, ret0[e], lg, n_ret, 256, mxu_small, F32 if small else BF16)
            new_k.append(k3.reshape(b, t, n_ret, HEAD_DIM))
            new_v.append(v3.reshape(b, t, n_ret, HEAD_DIM))
            if paged is None:
                o_b = _moba_prompt(z3, k3, v3, slopes, n_ret, 4)
            else:
                cache_k, cache_v, page_table = paged
                n_seq, n_pages = page_table.shape
                pt_flat = page_table.reshape(-1)
                ksum = _moba_blocksum(cache_k, e, pt_flat, n_seq, n_pages)
                sel = _moba_select(ksum, z3, 4)
                sel_flat = sel[..., :MOBA_TOPK].reshape(-1)
                o_b = _moba_sample(z3, k3, v3, cache_k, cache_v, e, pt_flat, sel_flat, slopes, n_pages, 4)
            x = _outproj(o_r.reshape(b * t, -1), o_b.reshape(b * t, -1), w_out, e, x, norm_g[l, 1],
                         row_tile)
            new_ret.append(s_new)
        else:
            o = l // 2
            wp = w_pool if small else w_pool_bf
            y3, p_new = _pool(x.reshape(b, t, d), None if pool0 is None else pool0[o],
                              norm_g[l, 0], norm_g[l, 1], wp, o, pool_scale[o], 256, mxu_small)
            x = y3.reshape(b * t, d)
            new_pool.append(p_new)
        x = _mlp(x, norm_g[l, 2], norm_g[l, 3], w_up, w_down, l, 2 * row_tile, 512)
    return (x.reshape(b, t, d), jnp.stack(new_k), jnp.stack(new_v), jnp.stack(new_ret),
            jnp.stack(new_pool))


def kernel(x_prompt, x_sample, cache_k, cache_v, state_ret, state_pool, page_table, w_in, w_out, w_pool,
           pool_scale, w_up, w_down, norm_g):
    n_ret = state_ret.shape[2]
    n_moba = cache_k.shape[3]
    assert n_moba == n_ret and w_in.shape[2] == 7 * n_ret * HEAD_DIM
    lg = jnp.log1p(-jnp.exp2(-5.0 - jnp.arange(n_ret, dtype=F32)))
    slopes = jnp.exp2(-8.0 * (jnp.arange(n_moba, dtype=F32) + 1.0) / n_moba)
    weights = (w_in.astype(BF16), w_out.astype(BF16), w_pool.astype(BF16), w_pool, pool_scale,
               w_up.astype(BF16), w_down.astype(BF16), norm_g)
    n_even = state_ret.shape[0]
    ret_zero = jnp.zeros((n_even, x_prompt.shape[0], n_ret, HEAD_DIM, HEAD_DIM), F32)

    y_p, k_p, v_p, ret_p, pool_p = _trunk(x_prompt, ret_zero, None, None, weights, lg, slopes,
                                          row_tile=512, small=False)
    y_s, k_s, v_s, ret_s, pool_s = _trunk(x_sample, state_ret, state_pool,
                                          (cache_k, cache_v, page_table), weights, lg, slopes,
                                          row_tile=512, small=True)
    return (y_p, y_s, k_p, v_p, k_s, v_s, ret_p, ret_s, pool_p, pool_s)
```

```python
import functools

import jax
import jax.numpy as jnp
from jax import lax
from jax.experimental import pallas as pl
from jax.experimental.pallas import tpu as pltpu

F32 = jnp.float32
BF16 = jnp.bfloat16

HEAD_DIM = 128
MOBA_BLOCK = 256
MOBA_TOPK = 3
POOL_WINDOWS = (2, 4, 8, 16)
POOL_HALO = 16
POOL_STATE = max(POOL_WINDOWS) - 1
EPS = 1e-6
NEG = -1e30
LOG2E = 1.4426950408889634
VMEM_LIMIT = 48 * 1024 * 1024
VMEM_LIMIT_BIG = 58 * 1024 * 1024

_NT = (((1,), (1,)), ((), ()))
_TN = (((0,), (0,)), ((), ()))


def _params(*sem, vmem=VMEM_LIMIT):
    return pltpu.CompilerParams(dimension_semantics=sem, vmem_limit_bytes=vmem)


def _tile(n, target, align):
    if n <= target:
        return n
    t = target - target % align
    while n % t:
        t -= align
    return t


def _rms(x, g):
    return x * lax.rsqrt(jnp.mean(x * x, axis=-1, keepdims=True) + EPS) * g


def _inproj_kernel(x_ref, g_ref, w_ref, z_ref, k_ref, v_ref, h_ref, *, n_main, n_kv):
    j = pl.program_id(1)

    @pl.when(j == 0)
    def _():
        h_ref[...] = _rms(x_ref[...], g_ref[...]).astype(BF16)

    def emit(o_ref):
        o_ref[...] = jnp.dot(h_ref[...], w_ref[...], preferred_element_type=F32)

    pl.when(j < n_main)(lambda: emit(z_ref))
    pl.when((j >= n_main) & (j < n_main + n_kv))(lambda: emit(k_ref))
    pl.when(j >= n_main + n_kv)(lambda: emit(v_ref))


def _inproj(x, g, w, layer, group_w, tm, tn):
    m, d = x.shape
    tm, tn = _tile(m, tm, 8), _tile(group_w, tn, 128)
    n_main, n_kv = 5 * group_w // tn, group_w // tn
    assert w.shape[2] == 7 * group_w
    return pl.pallas_call(
        functools.partial(_inproj_kernel, n_main=n_main, n_kv=n_kv),
        out_shape=(jax.ShapeDtypeStruct((m, 5 * group_w), F32),
                   jax.ShapeDtypeStruct((m, group_w), F32),
                   jax.ShapeDtypeStruct((m, group_w), F32)),
        grid=(m // tm, n_main + 2 * n_kv),
        in_specs=[pl.BlockSpec((tm, d), lambda i, j: (i, 0)),
                  pl.BlockSpec((1, d), lambda i, j: (0, 0)),
                  pl.BlockSpec((None, d, tn), lambda i, j: (layer, 0, j))],
        out_specs=(pl.BlockSpec((tm, tn), lambda i, j: (i, jnp.minimum(j, n_main - 1))),
                   pl.BlockSpec((tm, tn), lambda i, j: (i, jnp.clip(j - n_main, 0, n_kv - 1))),
                   pl.BlockSpec((tm, tn), lambda i, j: (i, jnp.clip(j - n_main - n_kv, 0, n_kv - 1)))),
        scratch_shapes=[pltpu.VMEM((tm, d), BF16)],
        compiler_params=_params("parallel", "arbitrary"),
        name="in_proj",
    )(x, g.reshape(1, d), w)


def _mlp_kernel(*refs, n_side, ppb):
    pages = refs[6:6 + n_side]
    x_ref, g2_ref, g3_ref, wu_ref, wd_ref = refs[1:6]
    if n_side:
        o_ref, ks_ref, h_ref = refs[6 + n_side:]
    else:
        o_ref, h_ref = refs[6:]
    f = pl.program_id(1)

    @pl.when(f == 0)
    def _():
        h_ref[...] = _rms(x_ref[...], g2_ref[...]).astype(BF16)
        o_ref[...] = jnp.zeros_like(o_ref)

    u = jnp.dot(h_ref[...], wu_ref[...], preferred_element_type=F32)
    a = jnp.square(jnp.maximum(u, 0.0)).astype(BF16)
    o_ref[...] += jnp.dot(a, wd_ref[...], preferred_element_type=F32)

    for blk in range(n_side // ppb):
        s = jnp.sum(pages[blk * ppb][...], axis=0)
        for r in range(1, ppb):
            s = s + jnp.sum(pages[blk * ppb + r][...], axis=0)
        ks_ref[blk] = s

    @pl.when(f == pl.num_programs(1) - 1)
    def _():
        o_ref[...] = x_ref[...] + _rms(o_ref[...], g3_ref[...])


def _mlp(x, g2, g3, w_up, w_down, layer, tm, tf, side=None):
    m, d = x.shape
    ff = w_up.shape[2]
    tm, tf = _tile(m, tm, 8), _tile(ff, tf, 128)
    nf = ff // tf
    steps = (m // tm) * nf
    in_specs = [pl.BlockSpec((tm, d), lambda i, f, pt: (i, 0)),
                pl.BlockSpec((1, d), lambda i, f, pt: (0, 0)),
                pl.BlockSpec((1, d), lambda i, f, pt: (0, 0)),
                pl.BlockSpec((None, d, tf), lambda i, f, pt: (layer, 0, f)),
                pl.BlockSpec((None, tf, d), lambda i, f, pt: (layer, f, 0))]
    out_shape = [jax.ShapeDtypeStruct((m, d), F32)]
    out_specs = [pl.BlockSpec((tm, d), lambda i, f, pt: (i, 0))]
    args = [x, g2.reshape(1, d), g3.reshape(1, d), w_up, w_down]
    n_side, ppb = 0, 1
    pt_flat = jnp.zeros((1,), jnp.int32)
    if side is not None:
        cache, cache_layer, pt_flat, first_page, n_side_pages = side
        _, _, page, n_heads, hd = cache.shape
        ppb = MOBA_BLOCK // page
        n_side = n_side_pages // steps
        assert n_side * steps == n_side_pages and n_side % ppb == 0 and first_page % ppb == 0

        def page_spec(r):
            return pl.BlockSpec(
                (None, None, page, n_heads, hd),
                lambda i, f, pt: (cache_layer, pt[first_page + (i * nf + f) * n_side + r], 0, 0, 0))

        in_specs += [page_spec(r) for r in range(n_side)]
        args += [cache] * n_side
        out_shape.append(jax.ShapeDtypeStruct((n_side_pages // ppb, n_heads, hd), F32))
        out_specs.append(pl.BlockSpec((n_side // ppb, n_heads, hd), lambda i, f, pt: (i * nf + f, 0, 0)))
    out = pl.pallas_call(
        functools.partial(_mlp_kernel, n_side=n_side, ppb=ppb),
        out_shape=tuple(out_shape),
        grid_spec=pltpu.PrefetchScalarGridSpec(
            num_scalar_prefetch=1,
            grid=(m // tm, nf),
            in_specs=in_specs,
            out_specs=tuple(out_specs),
            scratch_shapes=[pltpu.VMEM((tm, d), BF16)]),
        compiler_params=_params("parallel", "arbitrary", vmem=VMEM_LIMIT_BIG),
        name="mlp",
    )(pt_flat, *args)
    return out if side is not None else out[0]


def _outproj_kernel(or_ref, ob_ref, w_ref, x_ref, g_ref, o_ref, *, ret_w):
    m = jnp.dot(or_ref[...].astype(BF16), w_ref[0:ret_w, :], preferred_element_type=F32)
    m = m + jnp.dot(ob_ref[...].astype(BF16), w_ref[ret_w:, :], preferred_element_type=F32)
    o_ref[...] = x_ref[...] + _rms(m, g_ref[...])


def _outproj(o_r, o_b, w_out, layer, x, g, tm):
    m, d = x.shape
    ret_w, moba_w = o_r.shape[1], o_b.shape[1]
    tm = _tile(m, tm, 8)
    return pl.pallas_call(
        functools.partial(_outproj_kernel, ret_w=ret_w),
        out_shape=jax.ShapeDtypeStruct((m, d), F32),
        grid=(m // tm,),
        in_specs=[pl.BlockSpec((tm, ret_w), lambda i: (i, 0)),
                  pl.BlockSpec((tm, moba_w), lambda i: (i, 0)),
                  pl.BlockSpec((None, ret_w + moba_w, d), lambda i: (layer, 0, 0)),
                  pl.BlockSpec((tm, d), lambda i: (i, 0)),
                  pl.BlockSpec((1, d), lambda i: (0, 0))],
        out_specs=pl.BlockSpec((tm, d), lambda i: (i, 0)),
        compiler_params=_params("parallel"),
        name="outproj",
    )(o_r, o_b, w_out, x, g.reshape(1, d))


RET_HEADS_PER_STEP = 8


def _retention_kernel(lg_ref, q_ref, k_ref, v_ref, g_ref, s0_ref, o_ref, sfin_ref, s_ref, decay_ref,
                      *, chunk, hg, mxu_dtype):
    d = HEAD_DIM
    g = pl.program_id(1)
    c = pl.program_id(2)
    heads = range(hg)
    lgs = [lg_ref[g * hg + u] for u in heads]

    @pl.when(c == 0)
    def _():
        s_ref[...] = s0_ref[...]
        ii = lax.broadcasted_iota(jnp.int32, (chunk, chunk), 0)
        jj = lax.broadcasted_iota(jnp.int32, (chunk, chunk), 1)
        diff = (ii - jj).astype(F32)
        for u in heads:
            decay_ref[u] = jnp.where(diff >= 0, jnp.exp(lgs[u] * jnp.maximum(diff, 0.0)), 0.0)

    idx = lax.broadcasted_iota(jnp.int32, (chunk, 1), 0).astype(F32)
    cols = [slice(u * d, (u + 1) * d) for u in heads]
    ks = [k_ref[:, cols[u]] * (d ** -0.5) for u in heads]
    qm = [q_ref[:, cols[u]].astype(mxu_dtype) for u in heads]
    vm = [v_ref[:, cols[u]].astype(mxu_dtype) for u in heads]
    states = [s_ref[u] for u in heads]

    raw = [lax.dot_general(qm[u], ks[u].astype(mxu_dtype), _NT, preferred_element_type=F32)
           for u in heads]
    qs = [jnp.dot(qm[u], states[u].astype(mxu_dtype), preferred_element_type=F32) for u in heads]
    inner = [(raw[u] * decay_ref[u]).astype(mxu_dtype) for u in heads]
    kw = [(ks[u] * jnp.exp(lgs[u] * (chunk - 1.0 - idx))).astype(mxu_dtype) for u in heads]
    intra = [jnp.dot(inner[u], vm[u], preferred_element_type=F32) for u in heads]
    kv = [lax.dot_general(kw[u], vm[u], _TN, preferred_element_type=F32) for u in heads]

    last = c == pl.num_programs(2) - 1
    for u in heads:
        o = intra[u] + qs[u] * jnp.exp(lgs[u] * (idx + 1.0))
        s_new = jnp.exp(jnp.full((1, d), lgs[u] * chunk, F32)) * states[u] + kv[u]
        s_ref[u] = s_new
        mu = jnp.mean(o, axis=-1, keepdims=True)
        oc = o - mu
        var = jnp.mean(oc * oc, axis=-1, keepdims=True)
        gate = g_ref[:, cols[u]]
        silu = gate / (1.0 + jnp.exp(-gate))
        o_ref[:, cols[u]] = (oc * lax.rsqrt(var + EPS) * silu).astype(o_ref.dtype)

        @pl.when(last)
        def _():
            sfin_ref[u] = s_new


def _retention(z3, s0, lg, n_heads, chunk, mxu_dtype, out_dtype):
    b, t, _ = z3.shape
    d = HEAD_DIM
    chunk = _tile(t, chunk, 8)
    hg = _tile(n_heads, RET_HEADS_PER_STEP, 1)
    n_groups = n_heads // hg

    def col(group):
        return pl.BlockSpec((None, chunk, hg * d), lambda bi, g, c: (bi, c, group * n_groups + g))

    state_spec = pl.BlockSpec((None, hg, d, d), lambda bi, g, c: (bi, g, 0, 0))
    return pl.pallas_call(
        functools.partial(_retention_kernel, chunk=chunk, hg=hg, mxu_dtype=mxu_dtype),
        out_shape=(jax.ShapeDtypeStruct((b, t, n_heads * d), out_dtype),
                   jax.ShapeDtypeStruct((b, n_heads, d, d), F32)),
        grid=(b, n_groups, t // chunk),
        in_specs=[pl.BlockSpec(memory_space=pltpu.SMEM), col(0), col(1), col(2), col(3), state_spec],
        out_specs=(pl.BlockSpec((None, chunk, hg * d), lambda bi, g, c: (bi, c, g)), state_spec),
        scratch_shapes=[pltpu.VMEM((hg, d, d), F32), pltpu.VMEM((hg, chunk, chunk), F32)],
        compiler_params=_params("parallel", "parallel", "arbitrary"),
        name="retention",
    )(lg, z3, z3, z3, z3, s0)


def _topk_rank(bs, n_blocks, n_past, axis):
    idx = lax.broadcasted_iota(jnp.int32, bs.shape, axis)
    cnt = jnp.zeros(bs.shape, jnp.int32)
    for n in range(n_blocks):
        sc = bs[n:n + 1, :] if axis == 0 else bs[:, n:n + 1]
        beats = jnp.where(sc > bs, 1, jnp.where((sc == bs) & (n < idx), 1, 0))
        cnt = cnt + jnp.where(n < n_past, beats, 0)
    return cnt, idx


MOBA_HEADS_PER_STEP = 8


def _moba_prompt_kernel(slopes_ref, q_ref, k_ref, v_ref, o_ref, vt_ref, kmean_ref, bias_ref,
                        alibi_ref, *, nb, nbp, hg):
    blk, d = MOBA_BLOCK, HEAD_DIM
    g = pl.program_id(1)
    i = pl.program_id(2)
    heads = range(hg)

    @pl.when(i == 0)
    def _():
        kmean_ref[...] = jnp.zeros_like(kmean_ref)
        for u in heads:
            for n in range(nb):
                kblk = k_ref[n * blk:(n + 1) * blk, u * d:(u + 1) * d]
                kmean_ref[u, n:n + 1, :] = jnp.mean(kblk, axis=0, keepdims=True)
                vt_ref[u, n] = v_ref[n * blk:(n + 1) * blk, u * d:(u + 1) * d].T.astype(BF16)

    def kblock(u, n):
        return k_ref[pl.ds(pl.multiple_of(n * blk, blk), blk), u * d:(u + 1) * d].astype(BF16)

    kk = lax.broadcasted_iota(jnp.int32, (blk, blk), 0)
    qq = lax.broadcasted_iota(jnp.int32, (blk, blk), 1)
    slopes = [slopes_ref[g * hg + u] * LOG2E for u in heads]
    qtbs, carry0 = [], []
    for u in heads:
        qt = q_ref[:, u * d:(u + 1) * d].T
        bs = jnp.dot(kmean_ref[u], qt, precision=lax.Precision.HIGHEST,
                     preferred_element_type=F32)
        cnt, bidx = _topk_rank(bs, nbp, i, 0)
        bias_ref[u] = jnp.where((bidx < i) & (cnt < MOBA_TOPK), 0.0, NEG)
        qtb = (qt * (d ** -0.5 * LOG2E)).astype(BF16)
        alibi = slopes[u] * kk.astype(F32)
        alibi_ref[u] = alibi
        s = jnp.dot(kblock(u, i), qtb, preferred_element_type=F32)
        s = jnp.where(kk <= qq, s + alibi, NEG)
        m0 = jnp.max(s, axis=0, keepdims=True)
        p = jnp.exp2(s - m0)
        l0 = jnp.sum(p, axis=0, keepdims=True)
        acc0 = jnp.dot(vt_ref[u, i], p.astype(BF16), preferred_element_type=F32)
        qtbs.append(qtb)
        carry0 += [m0, l0, acc0]

    def body(j, carry):
        shift = ((i - j) * blk).astype(F32)
        raw = [jnp.dot(kblock(u, j), qtbs[u], preferred_element_type=F32) for u in heads]
        stats, probs = [], []
        for u in heads:
            m, l, _ = carry[3 * u:3 * u + 3]
            brow = bias_ref[u, pl.ds(j, 1), :] - slopes[u] * shift
            sj = raw[u] + alibi_ref[u] + brow
            m_new = jnp.maximum(m, jnp.max(sj, axis=0, keepdims=True))
            alpha = jnp.exp2(m - m_new)
            pj = jnp.exp2(sj - m_new)
            stats.append((m_new, alpha, alpha * l + jnp.sum(pj, axis=0, keepdims=True)))
            probs.append(pj.astype(BF16))
        out = []
        for u in heads:
            m_new, alpha, l_new = stats[u]
            pv = jnp.dot(vt_ref[u, j], probs[u], preferred_element_type=F32)
            out += [m_new, l_new, alpha * carry[3 * u + 2] + pv]
        return tuple(out)

    fin = lax.fori_loop(0, i, body, tuple(carry0))
    for u in heads:
        _, l, acc = fin[3 * u:3 * u + 3]
        o_ref[:, u * d:(u + 1) * d] = (acc / l).T.astype(o_ref.dtype)


def _moba_prompt(z3, k3, v3, slopes, n_heads, q_group):
    b, t, _ = z3.shape
    blk, d = MOBA_BLOCK, HEAD_DIM
    nb = t // blk
    nbp = -(-nb // 8) * 8
    hg = _tile(n_heads, MOBA_HEADS_PER_STEP, 1)
    n_groups = n_heads // hg
    whole = pl.BlockSpec((None, t, hg * d), lambda bi, g, i: (bi, 0, g), pipeline_mode=pl.Buffered(1))
    return pl.pallas_call(
        functools.partial(_moba_prompt_kernel, nb=nb, nbp=nbp, hg=hg),
        out_shape=jax.ShapeDtypeStruct((b, t, n_heads * d), BF16),
        grid=(b, n_groups, nb),
        in_specs=[pl.BlockSpec(memory_space=pltpu.SMEM),
                  pl.BlockSpec((None, blk, hg * d), lambda bi, g, i: (bi, i, q_group * n_groups + g)),
                  whole, whole],
        out_specs=pl.BlockSpec((None, blk, hg * d), lambda bi, g, i: (bi, i, g)),
        scratch_shapes=[pltpu.VMEM((hg, nb, d, blk), BF16),
                        pltpu.VMEM((hg, nbp, d), F32), pltpu.VMEM((hg, nbp, blk), F32),
                        pltpu.VMEM((hg, blk, blk), F32)],
        compiler_params=_params("parallel", "parallel", "arbitrary", vmem=VMEM_LIMIT_BIG),
        name="moba_prompt",
    )(slopes, z3, k3, v3)


def _moba_select_kernel(ks_ref, q_ref, o_ref, *, n_heads, nb):
    q = q_ref[...]
    lane = lax.broadcasted_iota(jnp.int32, (q.shape[0], HEAD_DIM), 1)
    for h in range(n_heads):
        kmean = ks_ref[h] * (1.0 / MOBA_BLOCK)
        bs = lax.dot_general(q[:, h * HEAD_DIM:(h + 1) * HEAD_DIM], kmean, _NT,
                             precision=lax.Precision.HIGHEST, preferred_element_type=F32)
        cnt, col = _topk_rank(bs, nb, nb, 1)
        out = jnp.zeros(lane.shape, jnp.int32)
        for r in range(MOBA_TOPK):
            idx = jnp.sum(jnp.where(cnt == r, col, 0).astype(F32), axis=1, keepdims=True)
            out = jnp.where(lane == r, idx.astype(jnp.int32), out)
        o_ref[h] = out


def _moba_select(ksum, z3, q_group):
    b, n_heads, nb, d = ksum.shape
    tq = z3.shape[1]
    assert nb >= MOBA_TOPK and tq <= MOBA_BLOCK
    return pl.pallas_call(
        functools.partial(_moba_select_kernel, n_heads=n_heads, nb=nb),
        out_shape=jax.ShapeDtypeStruct((b, n_heads, tq, HEAD_DIM), jnp.int32),
        grid=(b,),
        in_specs=[pl.BlockSpec((None, n_heads, nb, d), lambda bi: (bi, 0, 0, 0)),
                  pl.BlockSpec((None, tq, n_heads * d), lambda bi: (bi, 0, q_group))],
        out_specs=pl.BlockSpec((None, n_heads, tq, HEAD_DIM), lambda bi: (bi, 0, 0, 0)),
        compiler_params=_params("parallel"),
        name="moba_select",
    )(ksum, z3)


def _moba_sample_kernel(pt_ref, sel_ref, slopes_ref, q_ref, kn_ref, vn_ref, ck_ref, cv_ref, o_ref,
                        kbuf, vbuf, sem, *, layer, n_heads, n_pages, page, ppb, tq):
    b = pl.program_id(0)
    h = pl.program_id(1)
    step = b * n_heads + h
    n_steps = pl.num_programs(0) * n_heads
    npg = MOBA_TOPK * ppb
    slot = step % 2

    def copies(bi, hi, to_slot):
        out = []
        for t in range(tq):
            for n in range(npg):
                blk = sel_ref[((bi * n_heads + hi) * tq + t) * MOBA_TOPK + n // ppb]
                pg = pt_ref[bi * n_pages + blk * ppb + n % ppb]
                out.append(pltpu.make_async_copy(ck_ref.at[layer, pg, :, hi, :],
                                                 kbuf.at[to_slot, t * npg + n], sem.at[to_slot, 0]))
                out.append(pltpu.make_async_copy(cv_ref.at[layer, pg, :, hi, :],
                                                 vbuf.at[to_slot, t * npg + n], sem.at[to_slot, 1]))
        return out

    @pl.when(step == 0)
    def _():
        for cp in copies(b, h, slot):
            cp.start()

    @pl.when(step + 1 < n_steps)
    def _():
        nxt = step + 1
        for cp in copies(nxt // n_heads, nxt % n_heads, 1 - slot):
            cp.start()

    for cp in copies(b, h, slot):
        cp.wait()

    slope = slopes_ref[h]
    past_len = n_pages * page
    rowp = lax.broadcasted_iota(jnp.int32, (page, 1), 0)
    rowq = lax.broadcasted_iota(jnp.int32, (tq, 1), 0)
    kn = kn_ref[...]
    vn = vn_ref[...]

    def body(t, carry):
        q_t = q_ref[pl.ds(t, 1), :] * (HEAD_DIM ** -0.5)
        base = ((b * n_heads + h) * tq + t) * MOBA_TOPK
        scores = []
        for n in range(npg):
            kpos0 = sel_ref[base + n // ppb] * MOBA_BLOCK + (n % ppb) * page
            s = jnp.sum(kbuf[slot, t * npg + n] * q_t, axis=1, keepdims=True)
            dist = (past_len + t - kpos0 - rowp).astype(F32)
            scores.append(s - slope * dist)
        s_own = jnp.sum(kn * q_t, axis=1, keepdims=True)
        s_own = jnp.where(rowq <= t, s_own - slope * (t - rowq).astype(F32), NEG)
        m = jnp.max(s_own, axis=0, keepdims=True)
        for s in scores:
            m = jnp.maximum(m, jnp.max(s, axis=0, keepdims=True))
        p_own = jnp.exp(s_own - m)
        l = jnp.sum(p_own, axis=0, keepdims=True)
        acc = jnp.sum(p_own * vn, axis=0, keepdims=True)
        for n, s in enumerate(scores):
            p = jnp.exp(s - m)
            l = l + jnp.sum(p, axis=0, keepdims=True)
            acc = acc + jnp.sum(p * vbuf[slot, t * npg + n], axis=0, keepdims=True)
        o_ref[pl.ds(t, 1), :] = acc / l
        return carry

    lax.fori_loop(0, tq, body, 0)


def _moba_sample(z3, k3, v3, cache_k, cache_v, layer, pt_flat, sel_flat, slopes, n_pages, q_group):
    b, tq, _ = z3.shape
    _, _, page, n_heads, d = cache_k.shape
    ppb = MOBA_BLOCK // page
    n_bufs = tq * MOBA_TOPK * ppb
    new_rows = pl.BlockSpec((None, tq, d), lambda bi, h, pt, sel: (bi, 0, h))
    return pl.pallas_call(
        functools.partial(_moba_sample_kernel, layer=layer, n_heads=n_heads, n_pages=n_pages,
                          page=page, ppb=ppb, tq=tq),
        out_shape=jax.ShapeDtypeStruct((b, tq, n_heads * d), F32),
        grid_spec=pltpu.PrefetchScalarGridSpec(
            num_scalar_prefetch=2,
            grid=(b, n_heads),
            in_specs=[pl.BlockSpec(memory_space=pltpu.SMEM),
                      pl.BlockSpec((None, tq, d), lambda bi, h, pt, sel: (bi, 0, q_group * n_heads + h)),
                      new_rows, new_rows,
                      pl.BlockSpec(memory_space=pl.ANY), pl.BlockSpec(memory_space=pl.ANY)],
            out_specs=pl.BlockSpec((None, tq, d), lambda bi, h, pt, sel: (bi, 0, h)),
            scratch_shapes=[pltpu.VMEM((2, n_bufs, page, d), F32), pltpu.VMEM((2, n_bufs, page, d), F32),
                            pltpu.SemaphoreType.DMA((2, 2))]),
        compiler_params=_params("arbitrary", "arbitrary"),
        name="moba_sample",
    )(pt_flat, sel_flat, slopes, z3, k3, v3, cache_k, cache_v)


def _pool_kernel(x_ref, halo_ref, g0_ref, g1_ref, w_ref, sc_ref, y_ref, st_ref, ext_ref,
                 *, tm, group_w, halo_is_state, mxu_dtype):
    i = pl.program_id(1)
    x = x_ref[...]
    h = _rms(x, g0_ref[...])
    if halo_is_state:
        ext_ref[0:POOL_HALO - POOL_STATE, :] = jnp.zeros((POOL_HALO - POOL_STATE, x.shape[1]), F32)
        ext_ref[POOL_HALO - POOL_STATE:POOL_HALO, :] = halo_ref[...]
    else:
        ext_ref[0:POOL_HALO, :] = jnp.where(i > 0, _rms(halo_ref[...], g0_ref[...]), 0.0)
    ext_ref[POOL_HALO:POOL_HALO + tm, :] = h

    t = i * tm + lax.broadcasted_iota(jnp.int32, (tm, 1), 0)
    ys = []
    for gi, win in enumerate(POOL_WINDOWS):
        cols = slice(gi * group_w, (gi + 1) * group_w)
        acc = ext_ref[POOL_HALO:POOL_HALO + tm, cols]
        for s in range(1, win):
            acc = acc + ext_ref[POOL_HALO - s:POOL_HALO - s + tm, cols]
        if halo_is_state:
            mean = acc * (1.0 / win)
        else:
            mean = acc / jnp.minimum(t + 1, win).astype(F32)
        pooled = mean - h[:, cols]
        ys.append(jnp.dot(pooled.astype(mxu_dtype), w_ref[gi].astype(mxu_dtype),
                          preferred_element_type=F32))
    y = jnp.concatenate(ys, axis=1) * sc_ref[...]
    y_ref[...] = x + _rms(y, g1_ref[...])
    st_ref[...] = ext_ref[POOL_HALO + tm - POOL_STATE:POOL_HALO + tm, :]


def _pool(x3, state, g0, g1, w_pool, layer, scale, tm, mxu_dtype):
    b, t, d = x3.shape
    tm = _tile(t, tm, POOL_HALO)
    group_w = d // len(POOL_WINDOWS)
    halo_is_state = state is not None
    if halo_is_state:
        assert t == tm
        halo, halo_spec = state, pl.BlockSpec((None, POOL_STATE, d), lambda bi, i: (bi, 0, 0))
    else:
        per = tm // POOL_HALO
        halo = x3
        halo_spec = pl.BlockSpec((None, POOL_HALO, d), lambda bi, i: (bi, jnp.maximum(i * per - 1, 0), 0))
    vec = pl.BlockSpec((1, d), lambda bi, i: (0, 0))
    return pl.pallas_call(
        functools.partial(_pool_kernel, tm=tm, group_w=group_w, halo_is_state=halo_is_state,
                          mxu_dtype=mxu_dtype),
        out_shape=(jax.ShapeDtypeStruct((b, t, d), F32), jax.ShapeDtypeStruct((b, POOL_STATE, d), F32)),
        grid=(b, t // tm),
        in_specs=[pl.BlockSpec((None, tm, d), lambda bi, i: (bi, i, 0)), halo_spec, vec, vec,
                  pl.BlockSpec((None,) + w_pool.shape[1:], lambda bi, i: (layer, 0, 0, 0)), vec],
        out_specs=(pl.BlockSpec((None, tm, d), lambda bi, i: (bi, i, 0)),
                   pl.BlockSpec((None, POOL_STATE, d), lambda bi, i: (bi, 0, 0))),
        scratch_shapes=[pltpu.VMEM((POOL_HALO + tm, d), F32)],
        compiler_params=_params("parallel", "arbitrary"),
        name="pool",
    )(x3, halo, g0.reshape(1, d), g1.reshape(1, d), w_pool, scale.reshape(1, d))


def _trunk(x3, ret0, pool0, paged, weights, lg, slopes, *, row_tile, small, side_cache=None):
    w_in, w_out, w_pool_bf, w_pool, pool_scale, w_up, w_down, norm_g = weights
    b, t, d = x3.shape
    depth = w_up.shape[0]
    n_ret = ret0.shape[2]
    n_even = ret0.shape[0]
    group_w = n_ret * HEAD_DIM
    x = x3.reshape(b * t, d)
    mxu_small = F32 if small else BF16
    new_k, new_v, new_ret, new_pool, ksum_parts = [], [], [], [], []
    for l in range(depth):
        if l % 2 == 0:
            e = l // 2
            z, k_new, v_new = _inproj(x, norm_g[l, 0], w_in, e, group_w, 2 * row_tile, 512)
            z3 = z.reshape(b, t, -1)
            k3 = k_new.reshape(b, t, group_w)
            v3 = v_new.reshape(b, t, group_w)
            o_r, s_new = _retention(z3, ret0[e], lg, n_ret, 256, mxu_small, F32 if small else BF16)
            new_k.append(k3.reshape(b, t, n_ret, HEAD_DIM))
            new_v.append(v3.reshape(b, t, n_ret, HEAD_DIM))
            if paged is None:
                o_b = _moba_prompt(z3, k3, v3, slopes, n_ret, 4)
            else:
                cache_k, cache_v, page_table, ksums = paged
                n_seq, n_pages = page_table.shape
                pt_flat = page_table.reshape(-1)
                sel = _moba_select(ksums[e], z3, 4)
                sel_flat = sel[..., :MOBA_TOPK].reshape(-1)
                o_b = _moba_sample(z3, k3, v3, cache_k, cache_v, e, pt_flat, sel_flat, slopes, n_pages, 4)
            x = _outproj(o_r.reshape(b * t, -1), o_b.reshape(b * t, -1), w_out, e, x, norm_g[l, 1],
                         row_tile)
            new_ret.append(s_new)
        else:
            o = l // 2
            wp = w_pool if small else w_pool_bf
            y3, p_new = _pool(x.reshape(b, t, d), None if pool0 is None else pool0[o],
                              norm_g[l, 0], norm_g[l, 1], wp, o, pool_scale[o], 256, mxu_small)
            x = y3.reshape(b * t, d)
            new_pool.append(p_new)
        side = None
        if side_cache is not None:
            cache_k, page_table = side_cache
            per_layer = depth // n_even
            n_side_pages = page_table.size // per_layer
            side = (cache_k, l // per_layer, page_table.reshape(-1), (l % per_layer) * n_side_pages,
                    n_side_pages)
        x = _mlp(x, norm_g[l, 2], norm_g[l, 3], w_up, w_down, l, 2 * row_tile, 512, side)
        if side is not None:
            x, part = x
            ksum_parts.append(part)
    ksums = None
    if side_cache is not None:
        n_seq = side_cache[1].shape[0]
        per_layer = depth // n_even
        ksums = []
        for e in range(n_even):
            ks = jnp.concatenate(ksum_parts[e * per_layer:(e + 1) * per_layer], axis=0)
            ks = ks.reshape((n_seq, -1) + ks.shape[1:])
            ksums.append(ks.transpose(0, 2, 1, 3))
    return (x.reshape(b, t, d), jnp.stack(new_k), jnp.stack(new_v), jnp.stack(new_ret),
            jnp.stack(new_pool), ksums)


def kernel(x_prompt, x_sample, cache_k, cache_v, state_ret, state_pool, page_table, w_in, w_out, w_pool,
           pool_scale, w_up, w_down, norm_g):
    n_ret = state_ret.shape[2]
    n_moba = cache_k.shape[3]
    assert n_moba == n_ret and w_in.shape[2] == 7 * n_ret * HEAD_DIM
    lg = jnp.log1p(-jnp.exp2(-5.0 - jnp.arange(n_ret, dtype=F32)))
    slopes = jnp.exp2(-8.0 * (jnp.arange(n_moba, dtype=F32) + 1.0) / n_moba)
    weights = (w_in.astype(BF16), w_out.astype(BF16), w_pool.astype(BF16), w_pool, pool_scale,
               w_up.astype(BF16), w_down.astype(BF16), norm_g)
    n_even = state_ret.shape[0]
    ret_zero = jnp.zeros((n_even, x_prompt.shape[0], n_ret, HEAD_DIM, HEAD_DIM), F32)

    y_p, k_p, v_p, ret_p, pool_p, ksums = _trunk(x_prompt, ret_zero, None, None, weights, lg, slopes,
                                                 row_tile=512, small=False,
                                                 side_cache=(cache_k, page_table))
    y_s, k_s, v_s, ret_s, pool_s, _ = _trunk(x_sample, state_ret, state_pool,
                                             (cache_k, cache_v, page_table, ksums), weights, lg, slopes,
                                             row_tile=512, small=True)
    return (y_p, y_s, k_p, v_p, k_s, v_s, ret_p, ret_s, pool_p, pool_s)
```

```python
import functools

import jax
import jax.numpy as jnp
from jax import lax
from jax.experimental import pallas as pl
from jax.experimental.pallas import tpu as pltpu

F32 = jnp.float32
BF16 = jnp.bfloat16

HEAD_DIM = 128
MOBA_BLOCK = 256
MOBA_TOPK = 3
POOL_WINDOWS = (2, 4, 8, 16)
POOL_HALO = 32
assert POOL_WINDOWS == (2, 4, 8, 16)
POOL_STATE = max(POOL_WINDOWS) - 1
EPS = 1e-6
NEG = -1e30
LOG2E = 1.4426950408889634
VMEM_LIMIT = 48 * 1024 * 1024
VMEM_LIMIT_BIG = 58 * 1024 * 1024

_NT = (((1,), (1,)), ((), ()))
_TN = (((0,), (0,)), ((), ()))


def _params(*sem, vmem=VMEM_LIMIT):
    return pltpu.CompilerParams(dimension_semantics=sem, vmem_limit_bytes=vmem)


def _tile(n, target, align):
    if n <= target:
        return n
    t = target - target % align
    while n % t:
        t -= align
    return t


def _rms(x, g):
    return x * lax.rsqrt(jnp.mean(x * x, axis=-1, keepdims=True) + EPS) * g


def _inproj_kernel(x_ref, g_ref, w_ref, z_ref, k_ref, v_ref, h_ref, *, n_main, n_kv):
    j = pl.program_id(1)

    @pl.when(j == 0)
    def _():
        h_ref[...] = _rms(x_ref[...], g_ref[...]).astype(BF16)

    def emit(o_ref):
        o_ref[...] = jnp.dot(h_ref[...], w_ref[...], preferred_element_type=F32)

    pl.when(j < n_main)(lambda: emit(z_ref))
    pl.when((j >= n_main) & (j < n_main + n_kv))(lambda: emit(k_ref))
    pl.when(j >= n_main + n_kv)(lambda: emit(v_ref))


def _inproj(x, g, w, layer, group_w, tm, tn):
    m, d = x.shape
    tm, tn = _tile(m, tm, 8), _tile(group_w, tn, 128)
    n_main, n_kv = 5 * group_w // tn, group_w // tn
    assert w.shape[2] == 7 * group_w
    return pl.pallas_call(
        functools.partial(_inproj_kernel, n_main=n_main, n_kv=n_kv),
        out_shape=(jax.ShapeDtypeStruct((m, 5 * group_w), F32),
                   jax.ShapeDtypeStruct((m, group_w), F32),
                   jax.ShapeDtypeStruct((m, group_w), F32)),
        grid=(m // tm, n_main + 2 * n_kv),
        in_specs=[pl.BlockSpec((tm, d), lambda i, j: (i, 0)),
                  pl.BlockSpec((1, d), lambda i, j: (0, 0)),
                  pl.BlockSpec((None, d, tn), lambda i, j: (layer, 0, j))],
        out_specs=(pl.BlockSpec((tm, tn), lambda i, j: (i, jnp.minimum(j, n_main - 1))),
                   pl.BlockSpec((tm, tn), lambda i, j: (i, jnp.clip(j - n_main, 0, n_kv - 1))),
                   pl.BlockSpec((tm, tn), lambda i, j: (i, jnp.clip(j - n_main - n_kv, 0, n_kv - 1)))),
        scratch_shapes=[pltpu.VMEM((tm, d), BF16)],
        compiler_params=_params("parallel", "arbitrary"),
        name="in_proj",
    )(x, g.reshape(1, d), w)


def _mlp_kernel(*refs, n_side, ppb):
    pages = refs[6:6 + n_side]
    x_ref, g2_ref, g3_ref, wu_ref, wd_ref = refs[1:6]
    if n_side:
        o_ref, ks_ref, h_ref = refs[6 + n_side:]
    else:
        o_ref, h_ref = refs[6:]
    f = pl.program_id(1)

    @pl.when(f == 0)
    def _():
        h_ref[...] = _rms(x_ref[...], g2_ref[...]).astype(BF16)
        o_ref[...] = jnp.zeros_like(o_ref)

    u = jnp.dot(h_ref[...], wu_ref[...], preferred_element_type=F32)
    a = jnp.square(jnp.maximum(u, 0.0)).astype(BF16)
    o_ref[...] += jnp.dot(a, wd_ref[...], preferred_element_type=F32)

    for blk in range(n_side // ppb):
        s = jnp.sum(pages[blk * ppb][...], axis=0)
        for r in range(1, ppb):
            s = s + jnp.sum(pages[blk * ppb + r][...], axis=0)
        ks_ref[blk] = s

    @pl.when(f == pl.num_programs(1) - 1)
    def _():
        o_ref[...] = x_ref[...] + _rms(o_ref[...], g3_ref[...])


def _mlp(x, g2, g3, w_up, w_down, layer, tm, tf, side=None):
    m, d = x.shape
    ff = w_up.shape[2]
    tm, tf = _tile(m, tm, 8), _tile(ff, tf, 128)
    nf = ff // tf
    steps = (m // tm) * nf
    in_specs = [pl.BlockSpec((tm, d), lambda i, f, pt: (i, 0)),
                pl.BlockSpec((1, d), lambda i, f, pt: (0, 0)),
                pl.BlockSpec((1, d), lambda i, f, pt: (0, 0)),
                pl.BlockSpec((None, d, tf), lambda i, f, pt: (layer, 0, f)),
                pl.BlockSpec((None, tf, d), lambda i, f, pt: (layer, f, 0))]
    out_shape = [jax.ShapeDtypeStruct((m, d), F32)]
    out_specs = [pl.BlockSpec((tm, d), lambda i, f, pt: (i, 0))]
    args = [x, g2.reshape(1, d), g3.reshape(1, d), w_up, w_down]
    n_side, ppb = 0, 1
    pt_flat = jnp.zeros((1,), jnp.int32)
    if side is not None:
        cache, cache_layer, pt_flat, first_page, n_side_pages = side
        _, _, page, n_heads, hd = cache.shape
        ppb = MOBA_BLOCK // page
        n_side = n_side_pages // steps
        assert n_side * steps == n_side_pages and n_side % ppb == 0 and first_page % ppb == 0

        def page_spec(r):
            return pl.BlockSpec(
                (None, None, page, n_heads, hd),
                lambda i, f, pt: (cache_layer, pt[first_page + (i * nf + f) * n_side + r], 0, 0, 0))

        in_specs += [page_spec(r) for r in range(n_side)]
        args += [cache] * n_side
        out_shape.append(jax.ShapeDtypeStruct((n_side_pages // ppb, n_heads, hd), F32))
        out_specs.append(pl.BlockSpec((n_side // ppb, n_heads, hd), lambda i, f, pt: (i * nf + f, 0, 0)))
    out = pl.pallas_call(
        functools.partial(_mlp_kernel, n_side=n_side, ppb=ppb),
        out_shape=tuple(out_shape),
        grid_spec=pltpu.PrefetchScalarGridSpec(
            num_scalar_prefetch=1,
            grid=(m // tm, nf),
            in_specs=in_specs,
            out_specs=tuple(out_specs),
            scratch_shapes=[pltpu.VMEM((tm, d), BF16)]),
        compiler_params=_params("parallel", "arbitrary", vmem=VMEM_LIMIT_BIG),
        name="mlp",
    )(pt_flat, *args)
    return out if side is not None else out[0]


def _outproj_kernel(or_ref, ob_ref, w_ref, x_ref, g_ref, o_ref, *, ret_w):
    m = jnp.dot(or_ref[...].astype(BF16), w_ref[0:ret_w, :], preferred_element_type=F32)
    m = m + jnp.dot(ob_ref[...].astype(BF16), w_ref[ret_w:, :], preferred_element_type=F32)
    o_ref[...] = x_ref[...] + _rms(m, g_ref[...])


def _outproj(o_r, o_b, w_out, layer, x, g, tm):
    m, d = x.shape
    ret_w, moba_w = o_r.shape[1], o_b.shape[1]
    tm = _tile(m, tm, 8)
    return pl.pallas_call(
        functools.partial(_outproj_kernel, ret_w=ret_w),
        out_shape=jax.ShapeDtypeStruct((m, d), F32),
        grid=(m // tm,),
        in_specs=[pl.BlockSpec((tm, ret_w), lambda i: (i, 0)),
                  pl.BlockSpec((tm, moba_w), lambda i: (i, 0)),
                  pl.BlockSpec((None, ret_w + moba_w, d), lambda i: (layer, 0, 0)),
                  pl.BlockSpec((tm, d), lambda i: (i, 0)),
                  pl.BlockSpec((1, d), lambda i: (0, 0))],
        out_specs=pl.BlockSpec((tm, d), lambda i: (i, 0)),
        compiler_params=_params("parallel"),
        name="outproj",
    )(o_r, o_b, w_out, x, g.reshape(1, d))


RET_HEADS_PER_STEP = 8


def _retention_kernel(lg_ref, q_ref, k_ref, v_ref, g_ref, s0_ref, o_ref, sfin_ref, s_ref, decay_ref,
                      *, chunk, hg, mxu_dtype):
    d = HEAD_DIM
    g = pl.program_id(1)
    c = pl.program_id(2)
    heads = range(hg)
    lgs = [lg_ref[g * hg + u] for u in heads]

    @pl.when(c == 0)
    def _():
        s_ref[...] = s0_ref[...]
        ii = lax.broadcasted_iota(jnp.int32, (chunk, chunk), 0)
        jj = lax.broadcasted_iota(jnp.int32, (chunk, chunk), 1)
        diff = (ii - jj).astype(F32)
        for u in heads:
            decay_ref[u] = jnp.where(diff >= 0, jnp.exp(lgs[u] * jnp.maximum(diff, 0.0)), 0.0)

    idx = lax.broadcasted_iota(jnp.int32, (chunk, 1), 0).astype(F32)
    cols = [slice(u * d, (u + 1) * d) for u in heads]
    ks = [k_ref[:, cols[u]] * (d ** -0.5) for u in heads]
    qm = [q_ref[:, cols[u]].astype(mxu_dtype) for u in heads]
    vm = [v_ref[:, cols[u]].astype(mxu_dtype) for u in heads]
    states = [s_ref[u] for u in heads]

    raw = [lax.dot_general(qm[u], ks[u].astype(mxu_dtype), _NT, preferred_element_type=F32)
           for u in heads]
    qs = [jnp.dot(qm[u], states[u].astype(mxu_dtype), preferred_element_type=F32) for u in heads]
    inner = [(raw[u] * decay_ref[u]).astype(mxu_dtype) for u in heads]
    kw = [(ks[u] * jnp.exp(lgs[u] * (chunk - 1.0 - idx))).astype(mxu_dtype) for u in heads]
    intra = [jnp.dot(inner[u], vm[u], preferred_element_type=F32) for u in heads]
    kv = [lax.dot_general(kw[u], vm[u], _TN, preferred_element_type=F32) for u in heads]

    last = c == pl.num_programs(2) - 1
    for u in heads:
        o = intra[u] + qs[u] * jnp.exp(lgs[u] * (idx + 1.0))
        s_new = jnp.exp(jnp.full((1, d), lgs[u] * chunk, F32)) * states[u] + kv[u]
        s_ref[u] = s_new
        mu = jnp.mean(o, axis=-1, keepdims=True)
        oc = o - mu
        var = jnp.mean(oc * oc, axis=-1, keepdims=True)
        gate = g_ref[:, cols[u]]
        silu = gate / (1.0 + jnp.exp(-gate))
        o_ref[:, cols[u]] = (oc * lax.rsqrt(var + EPS) * silu).astype(o_ref.dtype)

        @pl.when(last)
        def _():
            sfin_ref[u] = s_new


def _retention(z3, s0, lg, n_heads, chunk, mxu_dtype, out_dtype):
    b, t, _ = z3.shape
    d = HEAD_DIM
    chunk = _tile(t, chunk, 8)
    hg = _tile(n_heads, RET_HEADS_PER_STEP, 1)
    n_groups = n_heads // hg

    def col(group):
        return pl.BlockSpec((None, chunk, hg * d), lambda bi, g, c: (bi, c, group * n_groups + g))

    state_spec = pl.BlockSpec((None, hg, d, d), lambda bi, g, c: (bi, g, 0, 0))
    return pl.pallas_call(
        functools.partial(_retention_kernel, chunk=chunk, hg=hg, mxu_dtype=mxu_dtype),
        out_shape=(jax.ShapeDtypeStruct((b, t, n_heads * d), out_dtype),
                   jax.ShapeDtypeStruct((b, n_heads, d, d), F32)),
        grid=(b, n_groups, t // chunk),
        in_specs=[pl.BlockSpec(memory_space=pltpu.SMEM), col(0), col(1), col(2), col(3), state_spec],
        out_specs=(pl.BlockSpec((None, chunk, hg * d), lambda bi, g, c: (bi, c, g)), state_spec),
        scratch_shapes=[pltpu.VMEM((hg, d, d), F32), pltpu.VMEM((hg, chunk, chunk), F32)],
        compiler_params=_params("parallel", "parallel", "arbitrary"),
        name="retention",
    )(lg, z3, z3, z3, z3, s0)


def _topk_rank(bs, n_blocks, n_past, axis):
    idx = lax.broadcasted_iota(jnp.int32, bs.shape, axis)
    cnt = jnp.zeros(bs.shape, jnp.int32)
    for n in range(n_blocks):
        sc = bs[n:n + 1, :] if axis == 0 else bs[:, n:n + 1]
        beats = jnp.where(sc > bs, 1, jnp.where((sc == bs) & (n < idx), 1, 0))
        cnt = cnt + jnp.where(n < n_past, beats, 0)
    return cnt, idx


MOBA_HEADS_PER_STEP = 8
SUM_ROWS = 16


def _moba_prompt_kernel(slopes_ref, q_ref, k_ref, v_ref, o_ref, vt_ref, kmean_ref, bias_ref,
                        alibi_ref, *, nb, nbp, hg):
    blk, d = MOBA_BLOCK, HEAD_DIM
    g = pl.program_id(1)
    i = pl.program_id(2)
    heads = range(hg)

    ones_row = jnp.where(lax.broadcasted_iota(jnp.int32, (SUM_ROWS, blk), 0) == 0, 1.0, 0.0).astype(BF16)

    @pl.when(i == 0)
    def _():
        kmean_ref[...] = jnp.zeros_like(kmean_ref)
        for u in heads:
            for n in range(nb):
                kblk = k_ref[n * blk:(n + 1) * blk, u * d:(u + 1) * d]
                kmean_ref[u, n:n + 1, :] = jnp.mean(kblk, axis=0, keepdims=True)
                vt_ref[u, n, 0:d, :] = v_ref[n * blk:(n + 1) * blk, u * d:(u + 1) * d].T.astype(BF16)
                vt_ref[u, n, d:, :] = ones_row

    def kblock(u, n):
        return k_ref[pl.ds(pl.multiple_of(n * blk, blk), blk), u * d:(u + 1) * d].astype(BF16)

    kk = lax.broadcasted_iota(jnp.int32, (blk, blk), 0)
    qq = lax.broadcasted_iota(jnp.int32, (blk, blk), 1)
    slopes = [slopes_ref[g * hg + u] * LOG2E for u in heads]
    qtbs, carry0 = [], []
    for u in heads:
        qt = q_ref[:, u * d:(u + 1) * d].T
        bs = jnp.dot(kmean_ref[u], qt, precision=lax.Precision.HIGHEST,
                     preferred_element_type=F32)
        cnt, bidx = _topk_rank(bs, nbp, i, 0)
        bias_ref[u] = jnp.where((bidx < i) & (cnt < MOBA_TOPK), 0.0, NEG)
        qtb = (qt * (d ** -0.5 * LOG2E)).astype(BF16)
        alibi = slopes[u] * kk.astype(F32)
        alibi_ref[u] = alibi
        s = jnp.dot(kblock(u, i), qtb, preferred_element_type=F32)
        s = jnp.where(kk <= qq, s + alibi, NEG)
        m0 = jnp.max(s, axis=0, keepdims=True)
        p = jnp.exp2(s - m0)
        acc0 = jnp.dot(vt_ref[u, i], p.astype(BF16), preferred_element_type=F32)
        qtbs.append(qtb)
        carry0 += [m0, acc0]

    def body(j, carry):
        shift = ((i - j) * blk).astype(F32)
        raw = [jnp.dot(kblock(u, j), qtbs[u], preferred_element_type=F32) for u in heads]
        stats, probs = [], []
        for u in heads:
            m = carry[2 * u]
            brow = bias_ref[u, pl.ds(j, 1), :] - slopes[u] * shift
            sj = raw[u] + alibi_ref[u]
            m_new = jnp.maximum(m, jnp.max(sj, axis=0, keepdims=True) + brow)
            stats.append((m_new, jnp.exp2(m - m_new)))
            probs.append(jnp.exp2(sj - (m_new - brow)).astype(BF16))
        out = []
        for u in heads:
            m_new, alpha = stats[u]
            pv = jnp.dot(vt_ref[u, j], probs[u], preferred_element_type=F32)
            out += [m_new, alpha * carry[2 * u + 1] + pv]
        return tuple(out)

    fin = lax.fori_loop(0, i, body, tuple(carry0))
    for u in heads:
        acc = fin[2 * u + 1]
        o_ref[:, u * d:(u + 1) * d] = (acc[0:d] / acc[d:d + 1]).T.astype(o_ref.dtype)


def _moba_prompt(z3, k3, v3, slopes, n_heads, q_group):
    b, t, _ = z3.shape
    blk, d = MOBA_BLOCK, HEAD_DIM
    nb = t // blk
    nbp = -(-nb // 8) * 8
    hg = _tile(n_heads, MOBA_HEADS_PER_STEP, 1)
    n_groups = n_heads // hg
    whole = pl.BlockSpec((None, t, hg * d), lambda bi, g, i: (bi, 0, g), pipeline_mode=pl.Buffered(1))
    return pl.pallas_call(
        functools.partial(_moba_prompt_kernel, nb=nb, nbp=nbp, hg=hg),
        out_shape=jax.ShapeDtypeStruct((b, t, n_heads * d), BF16),
        grid=(b, n_groups, nb),
        in_specs=[pl.BlockSpec(memory_space=pltpu.SMEM),
                  pl.BlockSpec((None, blk, hg * d), lambda bi, g, i: (bi, i, q_group * n_groups + g)),
                  whole, whole],
        out_specs=pl.BlockSpec((None, blk, hg * d), lambda bi, g, i: (bi, i, g)),
        scratch_shapes=[pltpu.VMEM((hg, nb, d + SUM_ROWS, blk), BF16),
                        pltpu.VMEM((hg, nbp, d), F32), pltpu.VMEM((hg, nbp, blk), F32),
                        pltpu.VMEM((hg, blk, blk), F32)],
        compiler_params=_params("parallel", "parallel", "arbitrary", vmem=VMEM_LIMIT_BIG),
        name="moba_prompt",
    )(slopes, z3, k3, v3)


def _moba_select_kernel(ks_ref, q_ref, o_ref, *, n_heads, nb):
    q = q_ref[...]
    lane = lax.broadcasted_iota(jnp.int32, (q.shape[0], HEAD_DIM), 1)
    for h in range(n_heads):
        kmean = ks_ref[h] * (1.0 / MOBA_BLOCK)
        bs = lax.dot_general(q[:, h * HEAD_DIM:(h + 1) * HEAD_DIM], kmean, _NT,
                             precision=lax.Precision.HIGHEST, preferred_element_type=F32)
        cnt, col = _topk_rank(bs, nb, nb, 1)
        out = jnp.zeros(lane.shape, jnp.int32)
        for r in range(MOBA_TOPK):
            idx = jnp.sum(jnp.where(cnt == r, col, 0).astype(F32), axis=1, keepdims=True)
            out = jnp.where(lane == r, idx.astype(jnp.int32), out)
        o_ref[h] = out


def _moba_select(ksum, z3, q_group):
    b, n_heads, nb, d = ksum.shape
    tq = z3.shape[1]
    assert nb >= MOBA_TOPK and tq <= MOBA_BLOCK
    return pl.pallas_call(
        functools.partial(_moba_select_kernel, n_heads=n_heads, nb=nb),
        out_shape=jax.ShapeDtypeStruct((b, n_heads, tq, HEAD_DIM), jnp.int32),
        grid=(b,),
        in_specs=[pl.BlockSpec((None, n_heads, nb, d), lambda bi: (bi, 0, 0, 0)),
                  pl.BlockSpec((None, tq, n_heads * d), lambda bi: (bi, 0, q_group))],
        out_specs=pl.BlockSpec((None, n_heads, tq, HEAD_DIM), lambda bi: (bi, 0, 0, 0)),
        compiler_params=_params("parallel"),
        name="moba_select",
    )(ksum, z3)


def _moba_sample_kernel(pt_ref, sel_ref, slopes_ref, q_ref, kn_ref, vn_ref, ck_ref, cv_ref, o_ref,
                        kbuf, vbuf, sem, *, layer, n_heads, n_pages, page, ppb, tq):
    b = pl.program_id(0)
    h = pl.program_id(1)
    step = b * n_heads + h
    n_steps = pl.num_programs(0) * n_heads
    npg = MOBA_TOPK * ppb
    slot = step % 2

    def copies(bi, hi, t, to_slot):
        out = []
        for n in range(npg):
            blk = sel_ref[((bi * n_heads + hi) * tq + t) * MOBA_TOPK + n // ppb]
            pg = pt_ref[bi * n_pages + blk * ppb + n % ppb]
            out.append(pltpu.make_async_copy(ck_ref.at[layer, pg, :, hi, :],
                                             kbuf.at[to_slot, t * npg + n], sem.at[to_slot, 0]))
            out.append(pltpu.make_async_copy(cv_ref.at[layer, pg, :, hi, :],
                                             vbuf.at[to_slot, t * npg + n], sem.at[to_slot, 1]))
        return out

    @pl.when(step == 0)
    def _():
        for t in range(tq):
            for cp in copies(b, h, t, slot):
                cp.start()

    for t in range(tq):
        for cp in copies(b, h, t, slot):
            cp.wait()
    nxt = jnp.minimum(step + 1, n_steps - 1)

    slope = slopes_ref[h]
    past_len = n_pages * page
    rowp = lax.broadcasted_iota(jnp.int32, (page, 1), 0)
    rowq = lax.broadcasted_iota(jnp.int32, (tq, 1), 0)
    kn = kn_ref[...]
    vn = vn_ref[...]

    def body(t, carry):
        for cp in copies(nxt // n_heads, nxt % n_heads, t, 1 - slot):
            cp.start()

        q_t = q_ref[pl.ds(t, 1), :] * (HEAD_DIM ** -0.5)
        base = ((b * n_heads + h) * tq + t) * MOBA_TOPK
        scores = []
        for n in range(npg):
            kpos0 = sel_ref[base + n // ppb] * MOBA_BLOCK + (n % ppb) * page
            s = jnp.sum(kbuf[slot, t * npg + n] * q_t, axis=1, keepdims=True)
            dist = (past_len + t - kpos0 - rowp).astype(F32)
            scores.append(s - slope * dist)
        s_own = jnp.sum(kn * q_t, axis=1, keepdims=True)
        s_own = jnp.where(rowq <= t, s_own - slope * (t - rowq).astype(F32), NEG)
        m = jnp.max(s_own, axis=0, keepdims=True)
        for s in scores:
            m = jnp.maximum(m, jnp.max(s, axis=0, keepdims=True))
        p_own = jnp.exp(s_own - m)
        l = jnp.sum(p_own, axis=0, keepdims=True)
        acc = jnp.sum(p_own * vn, axis=0, keepdims=True)
        for n, s in enumerate(scores):
            p = jnp.exp(s - m)
            l = l + jnp.sum(p, axis=0, keepdims=True)
            acc = acc + jnp.sum(p * vbuf[slot, t * npg + n], axis=0, keepdims=True)
        o_ref[pl.ds(t, 1), :] = acc / l
        return carry

    lax.fori_loop(0, tq, body, 0)

    @pl.when(step == n_steps - 1)
    def _():
        for t in range(tq):
            for cp in copies(b, h, t, 1 - slot):
                cp.wait()


def _moba_sample(z3, k3, v3, cache_k, cache_v, layer, pt_flat, sel_flat, slopes, n_pages, q_group):
    b, tq, _ = z3.shape
    _, _, page, n_heads, d = cache_k.shape
    ppb = MOBA_BLOCK // page
    n_bufs = tq * MOBA_TOPK * ppb
    new_rows = pl.BlockSpec((None, tq, d), lambda bi, h, pt, sel: (bi, 0, h))
    return pl.pallas_call(
        functools.partial(_moba_sample_kernel, layer=layer, n_heads=n_heads, n_pages=n_pages,
                          page=page, ppb=ppb, tq=tq),
        out_shape=jax.ShapeDtypeStruct((b, tq, n_heads * d), F32),
        grid_spec=pltpu.PrefetchScalarGridSpec(
            num_scalar_prefetch=2,
            grid=(b, n_heads),
            in_specs=[pl.BlockSpec(memory_space=pltpu.SMEM),
                      pl.BlockSpec((None, tq, d), lambda bi, h, pt, sel: (bi, 0, q_group * n_heads + h)),
                      new_rows, new_rows,
                      pl.BlockSpec(memory_space=pl.ANY), pl.BlockSpec(memory_space=pl.ANY)],
            out_specs=pl.BlockSpec((None, tq, d), lambda bi, h, pt, sel: (bi, 0, h)),
            scratch_shapes=[pltpu.VMEM((2, n_bufs, page, d), F32), pltpu.VMEM((2, n_bufs, page, d), F32),
                            pltpu.SemaphoreType.DMA((2, 2))]),
        compiler_params=_params("arbitrary", "arbitrary"),
        name="moba_sample",
    )(pt_flat, sel_flat, slopes, z3, k3, v3, cache_k, cache_v)


def _pool_kernel(x_ref, halo_ref, g0_ref, g1_ref, w_ref, sc_ref, y_ref, st_ref, ext_ref, a_ref, b_ref,
                 *, tm, group_w, halo_is_state, mxu_dtype):
    i = pl.program_id(1)
    x = x_ref[...]
    h = _rms(x, g0_ref[...])
    if halo_is_state:
        ext_ref[0:POOL_HALO - POOL_STATE, :] = jnp.zeros((POOL_HALO - POOL_STATE, x.shape[1]), F32)
        ext_ref[POOL_HALO - POOL_STATE:POOL_HALO, :] = halo_ref[...]
    else:
        ext_ref[0:POOL_HALO, :] = jnp.where(i > 0, _rms(halo_ref[...], g0_ref[...]), 0.0)
    ext_ref[POOL_HALO:POOL_HALO + tm, :] = h

    r_end = POOL_HALO + tm
    g1c, g2c, g3c = group_w, 2 * group_w, 3 * group_w
    a_ref[8:r_end, :] = ext_ref[8:r_end, :] + ext_ref[7:r_end - 1, :]
    b_ref[16:r_end, g1c:] = a_ref[16:r_end, g1c:] + a_ref[14:r_end - 2, g1c:]
    a_ref[24:r_end, g2c:] = b_ref[24:r_end, g2c:] + b_ref[20:r_end - 4, g2c:]
    sums = [a_ref[POOL_HALO:r_end, 0:g1c], b_ref[POOL_HALO:r_end, g1c:g2c],
            a_ref[POOL_HALO:r_end, g2c:g3c],
            a_ref[POOL_HALO:r_end, g3c:] + a_ref[POOL_HALO - 8:r_end - 8, g3c:]]

    t = i * tm + lax.broadcasted_iota(jnp.int32, (tm, 1), 0)
    ys = []
    for gi, win in enumerate(POOL_WINDOWS):
        cols = slice(gi * group_w, (gi + 1) * group_w)
        acc = sums[gi]
        if halo_is_state:
            mean = acc * (1.0 / win)
        else:
            mean = acc / jnp.minimum(t + 1, win).astype(F32)
        pooled = mean - h[:, cols]
        ys.append(jnp.dot(pooled.astype(mxu_dtype), w_ref[gi].astype(mxu_dtype),
                          preferred_element_type=F32))
    y = jnp.concatenate(ys, axis=1) * sc_ref[...]
    y_ref[...] = x + _rms(y, g1_ref[...])
    st_ref[...] = ext_ref[POOL_HALO + tm - POOL_STATE:POOL_HALO + tm, :]


def _pool(x3, state, g0, g1, w_pool, layer, scale, tm, mxu_dtype):
    b, t, d = x3.shape
    tm = _tile(t, tm, POOL_HALO)
    group_w = d // len(POOL_WINDOWS)
    halo_is_state = state is not None
    if halo_is_state:
        assert t == tm
        halo, halo_spec = state, pl.BlockSpec((None, POOL_STATE, d), lambda bi, i: (bi, 0, 0))
    else:
        per = tm // POOL_HALO
        halo = x3
        halo_spec = pl.BlockSpec((None, POOL_HALO, d), lambda bi, i: (bi, jnp.maximum(i * per - 1, 0), 0))
    vec = pl.BlockSpec((1, d), lambda bi, i: (0, 0))
    return pl.pallas_call(
        functools.partial(_pool_kernel, tm=tm, group_w=group_w, halo_is_state=halo_is_state,
                          mxu_dtype=mxu_dtype),
        out_shape=(jax.ShapeDtypeStruct((b, t, d), F32), jax.ShapeDtypeStruct((b, POOL_STATE, d), F32)),
        grid=(b, t // tm),
        in_specs=[pl.BlockSpec((None, tm, d), lambda bi, i: (bi, i, 0)), halo_spec, vec, vec,
                  pl.BlockSpec((None,) + w_pool.shape[1:], lambda bi, i: (layer, 0, 0, 0)), vec],
        out_specs=(pl.BlockSpec((None, tm, d), lambda bi, i: (bi, i, 0)),
                   pl.BlockSpec((None, POOL_STATE, d), lambda bi, i: (bi, 0, 0))),
        scratch_shapes=[pltpu.VMEM((POOL_HALO + tm, d), F32)] * 3,
        compiler_params=_params("parallel", "arbitrary"),
        name="pool",
    )(x3, halo, g0.reshape(1, d), g1.reshape(1, d), w_pool, scale.reshape(1, d))


def _trunk(x3, ret0, pool0, paged, weights, lg, slopes, *, row_tile, small, side_cache=None):
    w_in, w_out, w_pool_bf, w_pool, pool_scale, w_up, w_down, norm_g = weights
    b, t, d = x3.shape
    depth = w_up.shape[0]
    n_ret = ret0.shape[2]
    n_even = ret0.shape[0]
    group_w = n_ret * HEAD_DIM
    x = x3.reshape(b * t, d)
    mxu_small = F32 if small else BF16
    new_k, new_v, new_ret, new_pool, ksum_parts = [], [], [], [], []
    for l in range(depth):
        if l % 2 == 0:
            e = l // 2
            z, k_new, v_new = _inproj(x, norm_g[l, 0], w_in, e, group_w, 2 * row_tile, 512)
            z3 = z.reshape(b, t, -1)
            k3 = k_new.reshape(b, t, group_w)
            v3 = v_new.reshape(b, t, group_w)
            o_r, s_new = _retention(z3, ret0[e], lg, n_ret, 256, mxu_small, F32 if small else BF16)
            new_k.append(k3.reshape(b, t, n_ret, HEAD_DIM))
            new_v.append(v3.reshape(b, t, n_ret, HEAD_DIM))
            if paged is None:
                o_b = _moba_prompt(z3, k3, v3, slopes, n_ret, 4)
            else:
                cache_k, cache_v, page_table, ksums = paged
                n_seq, n_pages = page_table.shape
                pt_flat = page_table.reshape(-1)
                sel = _moba_select(ksums[e], z3, 4)
                sel_flat = sel[..., :MOBA_TOPK].reshape(-1)
                o_b = _moba_sample(z3, k3, v3, cache_k, cache_v, e, pt_flat, sel_flat, slopes, n_pages, 4)
            x = _outproj(o_r.reshape(b * t, -1), o_b.reshape(b * t, -1), w_out, e, x, norm_g[l, 1],
                         row_tile)
            new_ret.append(s_new)
        else:
            o = l // 2
            wp = w_pool if small else w_pool_bf
            y3, p_new = _pool(x.reshape(b, t, d), None if pool0 is None else pool0[o],
                              norm_g[l, 0], norm_g[l, 1], wp, o, pool_scale[o], 256, mxu_small)
            x = y3.reshape(b * t, d)
            new_pool.append(p_new)
        side = None
        if side_cache is not None:
            cache_k, page_table = side_cache
            per_layer = depth // n_even
            n_side_pages = page_table.size // per_layer
            side = (cache_k, l // per_layer, page_table.reshape(-1), (l % per_layer) * n_side_pages,
                    n_side_pages)
        x = _mlp(x, norm_g[l, 2], norm_g[l, 3], w_up, w_down, l, 2 * row_tile, 512, side)
        if side is not None:
            x, part = x
            ksum_parts.append(part)
    ksums = None
    if side_cache is not None:
        n_seq = side_cache[1].shape[0]
        per_layer = depth // n_even
        ksums = []
        for e in range(n_even):
            ks = jnp.concatenate(ksum_parts[e * per_layer:(e + 1) * per_layer], axis=0)
            ks = ks.reshape((n_seq, -1) + ks.shape[1:])
            ksums.append(ks.transpose(0, 2, 1, 3))
    return (x.reshape(b, t, d), jnp.stack(new_k), jnp.stack(new_v), jnp.stack(new_ret),
            jnp.stack(new_pool), ksums)


def kernel(x_prompt, x_sample, cache_k, cache_v, state_ret, state_pool, page_table, w_in, w_out, w_pool,
           pool_scale, w_up, w_down, norm_g):
    n_ret = state_ret.shape[2]
    n_moba = cache_k.shape[3]
    assert n_moba == n_ret and w_in.shape[2] == 7 * n_ret * HEAD_DIM
    lg = jnp.log1p(-jnp.exp2(-5.0 - jnp.arange(n_ret, dtype=F32)))
    slopes = jnp.exp2(-8.0 * (jnp.arange(n_moba, dtype=F32) + 1.0) / n_moba)
    weights = (w_in.astype(BF16), w_out.astype(BF16), w_pool.astype(BF16), w_pool, pool_scale,
               w_up.astype(BF16), w_down.astype(BF16), norm_g)
    n_even = state_ret.shape[0]
    ret_zero = jnp.zeros((n_even, x_prompt.shape[0], n_ret, HEAD_DIM, HEAD_DIM), F32)

    y_p, k_p, v_p, ret_p, pool_p, ksums = _trunk(x_prompt, ret_zero, None, None, weights, lg, slopes,
                                                 row_tile=512, small=False,
                                                 side_cache=(cache_k, page_table))
    y_s, k_s, v_s, ret_s, pool_s, _ = _trunk(x_sample, state_ret, state_pool,
                                             (cache_k, cache_v, page_table, ksums), weights, lg, slopes,
                                             row_tile=512, small=True)
    return (y_p, y_s, k_p, v_p, k_s, v_s, ret_p, ret_s, pool_p, pool_s)
```

```python
import functools

import jax
import jax.numpy as jnp
from jax import lax
from jax.experimental import pallas as pl
from jax.experimental.pallas import tpu as pltpu

F32 = jnp.float32
BF16 = jnp.bfloat16

HEAD_DIM = 128
MOBA_BLOCK = 256
MOBA_TOPK = 3
POOL_WINDOWS = (2, 4, 8, 16)
POOL_HALO = 32
assert POOL_WINDOWS == (2, 4, 8, 16)
POOL_STATE = max(POOL_WINDOWS) - 1
EPS = 1e-6
NEG = -1e30
LOG2E = 1.4426950408889634
VMEM_LIMIT = 48 * 1024 * 1024
VMEM_LIMIT_BIG = 58 * 1024 * 1024

_NT = (((1,), (1,)), ((), ()))
_TN = (((0,), (0,)), ((), ()))


def _params(*sem, vmem=VMEM_LIMIT):
    return pltpu.CompilerParams(dimension_semantics=sem, vmem_limit_bytes=vmem)


def _tile(n, target, align):
    if n <= target:
        return n
    t = target - target % align
    while n % t:
        t -= align
    return t


def _rms(x, g):
    return x * lax.rsqrt(jnp.mean(x * x, axis=-1, keepdims=True) + EPS) * g


def _inproj_kernel(x_ref, g_ref, w_ref, z_ref, k_ref, v_ref, h_ref, *, n_main, n_kv):
    j = pl.program_id(1)
    precision = lax.Precision.HIGHEST if w_ref.dtype == F32 else None

    @pl.when(j == 0)
    def _():
        h_ref[...] = _rms(x_ref[...], g_ref[...]).astype(h_ref.dtype)

    def emit(o_ref):
        o_ref[...] = jnp.dot(h_ref[...], w_ref[...], precision=precision, preferred_element_type=F32)

    pl.when(j < n_main)(lambda: emit(z_ref))
    pl.when((j >= n_main) & (j < n_main + n_kv))(lambda: emit(k_ref))
    pl.when(j >= n_main + n_kv)(lambda: emit(v_ref))


def _inproj(x, g, w, layer, group_w, tm, tn):
    m, d = x.shape
    tm, tn = _tile(m, tm, 8), _tile(group_w, tn, 128)
    n_main, n_kv = 5 * group_w // tn, group_w // tn
    assert w.shape[2] == 7 * group_w
    return pl.pallas_call(
        functools.partial(_inproj_kernel, n_main=n_main, n_kv=n_kv),
        out_shape=(jax.ShapeDtypeStruct((m, 5 * group_w), F32),
                   jax.ShapeDtypeStruct((m, group_w), F32),
                   jax.ShapeDtypeStruct((m, group_w), F32)),
        grid=(m // tm, n_main + 2 * n_kv),
        in_specs=[pl.BlockSpec((tm, d), lambda i, j: (i, 0)),
                  pl.BlockSpec((1, d), lambda i, j: (0, 0)),
                  pl.BlockSpec((None, d, tn), lambda i, j: (layer, 0, j))],
        out_specs=(pl.BlockSpec((tm, tn), lambda i, j: (i, jnp.minimum(j, n_main - 1))),
                   pl.BlockSpec((tm, tn), lambda i, j: (i, jnp.clip(j - n_main, 0, n_kv - 1))),
                   pl.BlockSpec((tm, tn), lambda i, j: (i, jnp.clip(j - n_main - n_kv, 0, n_kv - 1)))),
        scratch_shapes=[pltpu.VMEM((tm, d), w.dtype)],
        compiler_params=_params("parallel", "arbitrary"),
        name="in_proj",
    )(x, g.reshape(1, d), w)


def _mlp_kernel(*refs, n_side, ppb):
    pages = refs[6:6 + n_side]
    x_ref, g2_ref, g3_ref, wu_ref, wd_ref = refs[1:6]
    if n_side:
        o_ref, ks_ref, h_ref = refs[6 + n_side:]
    else:
        o_ref, h_ref = refs[6:]
    f = pl.program_id(1)

    @pl.when(f == 0)
    def _():
        h_ref[...] = _rms(x_ref[...], g2_ref[...]).astype(BF16)
        o_ref[...] = jnp.zeros_like(o_ref)

    u = jnp.dot(h_ref[...], wu_ref[...], preferred_element_type=F32)
    a = jnp.square(jnp.maximum(u, 0.0)).astype(BF16)
    o_ref[...] += jnp.dot(a, wd_ref[...], preferred_element_type=F32)

    for blk in range(n_side // ppb):
        s = jnp.sum(pages[blk * ppb][...], axis=0)
        for r in range(1, ppb):
            s = s + jnp.sum(pages[blk * ppb + r][...], axis=0)
        ks_ref[blk] = s

    @pl.when(f == pl.num_programs(1) - 1)
    def _():
        o_ref[...] = x_ref[...] + _rms(o_ref[...], g3_ref[...])


def _mlp(x, g2, g3, w_up, w_down, layer, tm, tf, side=None):
    m, d = x.shape
    ff = w_up.shape[2]
    tm, tf = _tile(m, tm, 8), _tile(ff, tf, 128)
    nf = ff // tf
    steps = (m // tm) * nf
    in_specs = [pl.BlockSpec((tm, d), lambda i, f, pt: (i, 0)),
                pl.BlockSpec((1, d), lambda i, f, pt: (0, 0)),
                pl.BlockSpec((1, d), lambda i, f, pt: (0, 0)),
                pl.BlockSpec((None, d, tf), lambda i, f, pt: (layer, 0, f)),
                pl.BlockSpec((None, tf, d), lambda i, f, pt: (layer, f, 0))]
    out_shape = [jax.ShapeDtypeStruct((m, d), F32)]
    out_specs = [pl.BlockSpec((tm, d), lambda i, f, pt: (i, 0))]
    args = [x, g2.reshape(1, d), g3.reshape(1, d), w_up, w_down]
    n_side, ppb = 0, 1
    pt_flat = jnp.zeros((1,), jnp.int32)
    if side is not None:
        cache, cache_layer, pt_flat, first_page, n_side_pages = side
        _, _, page, n_heads, hd = cache.shape
        ppb = MOBA_BLOCK // page
        n_side = n_side_pages // steps
        assert n_side * steps == n_side_pages and n_side % ppb == 0 and first_page % ppb == 0

        def page_spec(r):
            return pl.BlockSpec(
                (None, None, page, n_heads, hd),
                lambda i, f, pt: (cache_layer, pt[first_page + (i * nf + f) * n_side + r], 0, 0, 0))

        in_specs += [page_spec(r) for r in range(n_side)]
        args += [cache] * n_side
        out_shape.append(jax.ShapeDtypeStruct((n_side_pages // ppb, n_heads, hd), F32))
        out_specs.append(pl.BlockSpec((n_side // ppb, n_heads, hd), lambda i, f, pt: (i * nf + f, 0, 0)))
    out = pl.pallas_call(
        functools.partial(_mlp_kernel, n_side=n_side, ppb=ppb),
        out_shape=tuple(out_shape),
        grid_spec=pltpu.PrefetchScalarGridSpec(
            num_scalar_prefetch=1,
            grid=(m // tm, nf),
            in_specs=in_specs,
            out_specs=tuple(out_specs),
            scratch_shapes=[pltpu.VMEM((tm, d), BF16)]),
        compiler_params=_params("parallel", "arbitrary", vmem=VMEM_LIMIT_BIG),
        name="mlp",
    )(pt_flat, *args)
    return out if side is not None else out[0]


def _outproj_kernel(or_ref, ob_ref, w_ref, x_ref, g_ref, o_ref, *, ret_w):
    m = jnp.dot(or_ref[...].astype(BF16), w_ref[0:ret_w, :], preferred_element_type=F32)
    m = m + jnp.dot(ob_ref[...].astype(BF16), w_ref[ret_w:, :], preferred_element_type=F32)
    o_ref[...] = x_ref[...] + _rms(m, g_ref[...])


def _outproj(o_r, o_b, w_out, layer, x, g, tm):
    m, d = x.shape
    ret_w, moba_w = o_r.shape[1], o_b.shape[1]
    tm = _tile(m, tm, 8)
    return pl.pallas_call(
        functools.partial(_outproj_kernel, ret_w=ret_w),
        out_shape=jax.ShapeDtypeStruct((m, d), F32),
        grid=(m // tm,),
        in_specs=[pl.BlockSpec((tm, ret_w), lambda i: (i, 0)),
                  pl.BlockSpec((tm, moba_w), lambda i: (i, 0)),
                  pl.BlockSpec((None, ret_w + moba_w, d), lambda i: (layer, 0, 0)),
                  pl.BlockSpec((tm, d), lambda i: (i, 0)),
                  pl.BlockSpec((1, d), lambda i: (0, 0))],
        out_specs=pl.BlockSpec((tm, d), lambda i: (i, 0)),
        compiler_params=_params("parallel"),
        name="outproj",
    )(o_r, o_b, w_out, x, g.reshape(1, d))


RET_HEADS_PER_STEP = 8


def _retention_kernel(*refs, chunk, hg, mxu_dtype, n_cast):
    lg_ref, q_ref, k_ref, v_ref, g_ref, s0_ref = refs[:6]
    cast_in = refs[6:6 + n_cast]
    o_ref, sfin_ref = refs[6 + n_cast:8 + n_cast]
    cast_out = refs[8 + n_cast:8 + 2 * n_cast]
    s_ref, decay_ref = refs[8 + 2 * n_cast:]
    for src, dst in zip(cast_in, cast_out):
        dst[...] = src[...].astype(dst.dtype)
    d = HEAD_DIM
    g = pl.program_id(1)
    c = pl.program_id(2)
    heads = range(hg)
    lgs = [lg_ref[g * hg + u] for u in heads]

    @pl.when(c == 0)
    def _():
        s_ref[...] = s0_ref[...]
        ii = lax.broadcasted_iota(jnp.int32, (chunk, chunk), 0)
        jj = lax.broadcasted_iota(jnp.int32, (chunk, chunk), 1)
        diff = (ii - jj).astype(F32)
        for u in heads:
            decay_ref[u] = jnp.where(diff >= 0, jnp.exp(lgs[u] * jnp.maximum(diff, 0.0)), 0.0)

    idx = lax.broadcasted_iota(jnp.int32, (chunk, 1), 0).astype(F32)
    cols = [slice(u * d, (u + 1) * d) for u in heads]
    ks = [k_ref[:, cols[u]] * (d ** -0.5) for u in heads]
    qm = [q_ref[:, cols[u]].astype(mxu_dtype) for u in heads]
    vm = [v_ref[:, cols[u]].astype(mxu_dtype) for u in heads]
    states = [s_ref[u] for u in heads]

    raw = [lax.dot_general(qm[u], ks[u].astype(mxu_dtype), _NT, preferred_element_type=F32)
           for u in heads]
    qs = [jnp.dot(qm[u], states[u].astype(mxu_dtype), preferred_element_type=F32) for u in heads]
    inner = [(raw[u] * decay_ref[u]).astype(mxu_dtype) for u in heads]
    kw = [(ks[u] * jnp.exp(lgs[u] * (chunk - 1.0 - idx))).astype(mxu_dtype) for u in heads]
    intra = [jnp.dot(inner[u], vm[u], preferred_element_type=F32) for u in heads]
    kv = [lax.dot_general(kw[u], vm[u], _TN, preferred_element_type=F32) for u in heads]

    last = c == pl.num_programs(2) - 1
    for u in heads:
        o = intra[u] + qs[u] * jnp.exp(lgs[u] * (idx + 1.0))
        s_new = jnp.exp(jnp.full((1, d), lgs[u] * chunk, F32)) * states[u] + kv[u]
        s_ref[u] = s_new
        mu = jnp.mean(o, axis=-1, keepdims=True)
        oc = o - mu
        var = jnp.mean(oc * oc, axis=-1, keepdims=True)
        gate = g_ref[:, cols[u]]
        silu = gate / (1.0 + jnp.exp(-gate))
        o_ref[:, cols[u]] = (oc * lax.rsqrt(var + EPS) * silu).astype(o_ref.dtype)

        @pl.when(last)
        def _():
            sfin_ref[u] = s_new


def _retention(z3, s0, lg, n_heads, chunk, mxu_dtype, out_dtype, cast=None):
    b, t, _ = z3.shape
    d = HEAD_DIM
    chunk = _tile(t, chunk, 8)
    hg = _tile(n_heads, RET_HEADS_PER_STEP, 1)
    n_groups = n_heads // hg
    nc = t // chunk
    steps = b * n_groups * nc

    def col(group):
        return pl.BlockSpec((None, chunk, hg * d), lambda bi, g, c: (bi, c, group * n_groups + g))

    state_spec = pl.BlockSpec((None, hg, d, d), lambda bi, g, c: (bi, g, 0, 0))
    in_specs = [pl.BlockSpec(memory_space=pltpu.SMEM), col(0), col(1), col(2), col(3), state_spec]
    out_shape = [jax.ShapeDtypeStruct((b, t, n_heads * d), out_dtype),
                 jax.ShapeDtypeStruct((b, n_heads, d, d), F32)]
    out_specs = [pl.BlockSpec((None, chunk, hg * d), lambda bi, g, c: (bi, c, g)), state_spec]
    args = [lg, z3, z3, z3, z3, s0]
    n_cast = 0
    if cast is not None and cast[0].shape[2] % (steps * 128) == 0:
        w_up, w_down, layer = cast
        _, dm, ff = w_up.shape
        tf = ff // steps
        n_cast = 2

        def step(bi, g, c):
            return (bi * n_groups + g) * nc + c

        in_specs += [pl.BlockSpec((None, dm, tf), lambda bi, g, c: (layer, 0, step(bi, g, c))),
                     pl.BlockSpec((None, tf, dm), lambda bi, g, c: (layer, step(bi, g, c), 0))]
        out_shape += [jax.ShapeDtypeStruct((dm, ff), BF16), jax.ShapeDtypeStruct((ff, dm), BF16)]
        out_specs += [pl.BlockSpec((dm, tf), lambda bi, g, c: (0, step(bi, g, c))),
                      pl.BlockSpec((tf, dm), lambda bi, g, c: (step(bi, g, c), 0))]
        args += [w_up, w_down]
    out = pl.pallas_call(
        functools.partial(_retention_kernel, chunk=chunk, hg=hg, mxu_dtype=mxu_dtype, n_cast=n_cast),
        out_shape=tuple(out_shape),
        grid=(b, n_groups, nc),
        in_specs=in_specs,
        out_specs=tuple(out_specs),
        scratch_shapes=[pltpu.VMEM((hg, d, d), F32), pltpu.VMEM((hg, chunk, chunk), F32)],
        compiler_params=_params("parallel", "parallel", "arbitrary"),
        name="retention",
    )(*args)
    return (out[0], out[1], (out[2], out[3]) if n_cast else None)


def _topk_rank(bs, n_blocks, n_past, axis):
    idx = lax.broadcasted_iota(jnp.int32, bs.shape, axis)
    cnt = jnp.zeros(bs.shape, jnp.int32)
    for n in range(n_blocks):
        sc = bs[n:n + 1, :] if axis == 0 else bs[:, n:n + 1]
        beats = jnp.where(sc > bs, 1, jnp.where((sc == bs) & (n < idx), 1, 0))
        cnt = cnt + jnp.where(n < n_past, beats, 0)
    return cnt, idx


MOBA_HEADS_PER_STEP = 8
SUM_ROWS = 16


def _moba_prompt_kernel(slopes_ref, q_ref, k_ref, v_ref, o_ref, vt_ref, kmean_ref, bias_ref,
                        alibi_ref, *, nb, nbp, hg):
    blk, d = MOBA_BLOCK, HEAD_DIM
    g = pl.program_id(1)
    i = pl.program_id(2)
    heads = range(hg)

    ones_row = jnp.where(lax.broadcasted_iota(jnp.int32, (SUM_ROWS, blk), 0) == 0, 1.0, 0.0).astype(BF16)

    @pl.when(i == 0)
    def _():
        kmean_ref[...] = jnp.zeros_like(kmean_ref)
        for u in heads:
            for n in range(nb):
                kblk = k_ref[n * blk:(n + 1) * blk, u * d:(u + 1) * d]
                kmean_ref[u, n:n + 1, :] = jnp.mean(kblk, axis=0, keepdims=True)
                vt_ref[u, n, 0:d, :] = v_ref[n * blk:(n + 1) * blk, u * d:(u + 1) * d].T.astype(BF16)
                vt_ref[u, n, d:, :] = ones_row

    def kblock(u, n):
        return k_ref[pl.ds(pl.multiple_of(n * blk, blk), blk), u * d:(u + 1) * d].astype(BF16)

    kk = lax.broadcasted_iota(jnp.int32, (blk, blk), 0)
    qq = lax.broadcasted_iota(jnp.int32, (blk, blk), 1)
    slopes = [slopes_ref[g * hg + u] * LOG2E for u in heads]
    qtbs, carry0 = [], []
    for u in heads:
        qt = q_ref[:, u * d:(u + 1) * d].T
        bs = jnp.dot(kmean_ref[u], qt, precision=lax.Precision.HIGHEST,
                     preferred_element_type=F32)
        cnt, bidx = _topk_rank(bs, nbp, i, 0)
        bias_ref[u] = jnp.where((bidx < i) & (cnt < MOBA_TOPK), 0.0, NEG)
        qtb = (qt * (d ** -0.5 * LOG2E)).astype(BF16)
        alibi = slopes[u] * kk.astype(F32)
        alibi_ref[u] = alibi
        s = jnp.dot(kblock(u, i), qtb, preferred_element_type=F32)
        s = jnp.where(kk <= qq, s + alibi, NEG)
        m0 = jnp.max(s, axis=0, keepdims=True)
        p = jnp.exp2(s - m0)
        acc0 = jnp.dot(vt_ref[u, i], p.astype(BF16), preferred_element_type=F32)
        qtbs.append(qtb)
        carry0 += [m0, acc0]

    def body(j, carry):
        shift = ((i - j) * blk).astype(F32)
        raw = [jnp.dot(kblock(u, j), qtbs[u], preferred_element_type=F32) for u in heads]
        stats, probs = [], []
        for u in heads:
            m = carry[2 * u]
            brow = bias_ref[u, pl.ds(j, 1), :] - slopes[u] * shift
            sj = raw[u] + alibi_ref[u]
            m_new = jnp.maximum(m, jnp.max(sj, axis=0, keepdims=True) + brow)
            stats.append((m_new, jnp.exp2(m - m_new)))
            probs.append(jnp.exp2(sj - (m_new - brow)).astype(BF16))
        out = []
        for u in heads:
            m_new, alpha = stats[u]
            pv = jnp.dot(vt_ref[u, j], probs[u], preferred_element_type=F32)
            out += [m_new, alpha * carry[2 * u + 1] + pv]
        return tuple(out)

    fin = lax.fori_loop(0, i, body, tuple(carry0))
    for u in heads:
        acc = fin[2 * u + 1]
        o_ref[:, u * d:(u + 1) * d] = (acc[0:d] / acc[d:d + 1]).T.astype(o_ref.dtype)


def _moba_prompt(z3, k3, v3, slopes, n_heads, q_group):
    b, t, _ = z3.shape
    blk, d = MOBA_BLOCK, HEAD_DIM
    nb = t // blk
    nbp = -(-nb // 8) * 8
    hg = _tile(n_heads, MOBA_HEADS_PER_STEP, 1)
    n_groups = n_heads // hg
    whole = pl.BlockSpec((None, t, hg * d), lambda bi, g, i: (bi, 0, g), pipeline_mode=pl.Buffered(1))
    return pl.pallas_call(
        functools.partial(_moba_prompt_kernel, nb=nb, nbp=nbp, hg=hg),
        out_shape=jax.ShapeDtypeStruct((b, t, n_heads * d), BF16),
        grid=(b, n_groups, nb),
        in_specs=[pl.BlockSpec(memory_space=pltpu.SMEM),
                  pl.BlockSpec((None, blk, hg * d), lambda bi, g, i: (bi, i, q_group * n_groups + g)),
                  whole, whole],
        out_specs=pl.BlockSpec((None, blk, hg * d), lambda bi, g, i: (bi, i, g)),
        scratch_shapes=[pltpu.VMEM((hg, nb, d + SUM_ROWS, blk), BF16),
                        pltpu.VMEM((hg, nbp, d), F32), pltpu.VMEM((hg, nbp, blk), F32),
                        pltpu.VMEM((hg, blk, blk), F32)],
        compiler_params=_params("parallel", "parallel", "arbitrary", vmem=VMEM_LIMIT_BIG),
        name="moba_prompt",
    )(slopes, z3, k3, v3)


def _moba_select_kernel(ks_ref, q_ref, o_ref, *, n_heads, nb):
    q = q_ref[...]
    lane = lax.broadcasted_iota(jnp.int32, (q.shape[0], HEAD_DIM), 1)
    for h in range(n_heads):
        kmean = ks_ref[h] * (1.0 / MOBA_BLOCK)
        bs = lax.dot_general(q[:, h * HEAD_DIM:(h + 1) * HEAD_DIM], kmean, _NT,
                             precision=lax.Precision.HIGHEST, preferred_element_type=F32)
        cnt, col = _topk_rank(bs, nb, nb, 1)
        out = jnp.zeros(lane.shape, jnp.int32)
        for r in range(MOBA_TOPK):
            idx = jnp.sum(jnp.where(cnt == r, col, 0).astype(F32), axis=1, keepdims=True)
            out = jnp.where(lane == r, idx.astype(jnp.int32), out)
        o_ref[h] = out


def _moba_select(ksum, z3, q_group):
    b, n_heads, nb, d = ksum.shape
    tq = z3.shape[1]
    assert nb >= MOBA_TOPK and tq <= MOBA_BLOCK
    return pl.pallas_call(
        functools.partial(_moba_select_kernel, n_heads=n_heads, nb=nb),
        out_shape=jax.ShapeDtypeStruct((b, n_heads, tq, HEAD_DIM), jnp.int32),
        grid=(b,),
        in_specs=[pl.BlockSpec((None, n_heads, nb, d), lambda bi: (bi, 0, 0, 0)),
                  pl.BlockSpec((None, tq, n_heads * d), lambda bi: (bi, 0, q_group))],
        out_specs=pl.BlockSpec((None, n_heads, tq, HEAD_DIM), lambda bi: (bi, 0, 0, 0)),
        compiler_params=_params("parallel"),
        name="moba_select",
    )(ksum, z3)


def _moba_sample_kernel(pt_ref, sel_ref, slopes_ref, q_ref, kn_ref, vn_ref, ck_ref, cv_ref, o_ref,
                        kbuf, vbuf, sem, *, layer, n_heads, n_pages, page, ppb, tq):
    b = pl.program_id(0)
    h = pl.program_id(1)
    step = b * n_heads + h
    n_steps = pl.num_programs(0) * n_heads
    npg = MOBA_TOPK * ppb
    slot = step % 2

    def copies(bi, hi, to_slot):
        out = []
        for t in range(tq):
            for n in range(npg):
                blk = sel_ref[((bi * n_heads + hi) * tq + t) * MOBA_TOPK + n // ppb]
                pg = pt_ref[bi * n_pages + blk * ppb + n % ppb]
                out.append(pltpu.make_async_copy(ck_ref.at[layer, pg, :, hi, :],
                                                 kbuf.at[to_slot, t * npg + n], sem.at[to_slot, 0]))
                out.append(pltpu.make_async_copy(cv_ref.at[layer, pg, :, hi, :],
                                                 vbuf.at[to_slot, t * npg + n], sem.at[to_slot, 1]))
        return out

    @pl.when(step == 0)
    def _():
        for cp in copies(b, h, slot):
            cp.start()

    @pl.when(step + 1 < n_steps)
    def _():
        nxt = step + 1
        for cp in copies(nxt // n_heads, nxt % n_heads, 1 - slot):
            cp.start()

    for cp in copies(b, h, slot):
        cp.wait()

    slope = slopes_ref[h]
    past_len = n_pages * page
    rowp = lax.broadcasted_iota(jnp.int32, (page, 1), 0)
    rowq = lax.broadcasted_iota(jnp.int32, (tq, 1), 0)
    kn = kn_ref[...]
    vn = vn_ref[...]

    def body(t, carry):
        q_t = q_ref[pl.ds(t, 1), :] * (HEAD_DIM ** -0.5)
        base = ((b * n_heads + h) * tq + t) * MOBA_TOPK
        scores = []
        for n in range(npg):
            kpos0 = sel_ref[base + n // ppb] * MOBA_BLOCK + (n % ppb) * page
            s = jnp.sum(kbuf[slot, t * npg + n] * q_t, axis=1, keepdims=True)
            dist = (past_len + t - kpos0 - rowp).astype(F32)
            scores.append(s - slope * dist)
        s_own = jnp.sum(kn * q_t, axis=1, keepdims=True)
        s_own = jnp.where(rowq <= t, s_own - slope * (t - rowq).astype(F32), NEG)
        m = jnp.max(s_own, axis=0, keepdims=True)
        for s in scores:
            m = jnp.maximum(m, jnp.max(s, axis=0, keepdims=True))
        p_own = jnp.exp(s_own - m)
        l = jnp.sum(p_own, axis=0, keepdims=True)
        acc = jnp.sum(p_own * vn, axis=0, keepdims=True)
        for n, s in enumerate(scores):
            p = jnp.exp(s - m)
            l = l + jnp.sum(p, axis=0, keepdims=True)
            acc = acc + jnp.sum(p * vbuf[slot, t * npg + n], axis=0, keepdims=True)
        o_ref[pl.ds(t, 1), :] = acc / l
        return carry

    lax.fori_loop(0, tq, body, 0)


def _moba_sample(z3, k3, v3, cache_k, cache_v, layer, pt_flat, sel_flat, slopes, n_pages, q_group):
    b, tq, _ = z3.shape
    _, _, page, n_heads, d = cache_k.shape
    ppb = MOBA_BLOCK // page
    n_bufs = tq * MOBA_TOPK * ppb
    new_rows = pl.BlockSpec((None, tq, d), lambda bi, h, pt, sel: (bi, 0, h))
    return pl.pallas_call(
        functools.partial(_moba_sample_kernel, layer=layer, n_heads=n_heads, n_pages=n_pages,
                          page=page, ppb=ppb, tq=tq),
        out_shape=jax.ShapeDtypeStruct((b, tq, n_heads * d), F32),
        grid_spec=pltpu.PrefetchScalarGridSpec(
            num_scalar_prefetch=2,
            grid=(b, n_heads),
            in_specs=[pl.BlockSpec(memory_space=pltpu.SMEM),
                      pl.BlockSpec((None, tq, d), lambda bi, h, pt, sel: (bi, 0, q_group * n_heads + h)),
                      new_rows, new_rows,
                      pl.BlockSpec(memory_space=pl.ANY), pl.BlockSpec(memory_space=pl.ANY)],
            out_specs=pl.BlockSpec((None, tq, d), lambda bi, h, pt, sel: (bi, 0, h)),
            scratch_shapes=[pltpu.VMEM((2, n_bufs, page, d), F32), pltpu.VMEM((2, n_bufs, page, d), F32),
                            pltpu.SemaphoreType.DMA((2, 2))]),
        compiler_params=_params("arbitrary", "arbitrary"),
        name="moba_sample",
    )(pt_flat, sel_flat, slopes, z3, k3, v3, cache_k, cache_v)


def _pool_kernel(x_ref, halo_ref, g0_ref, g1_ref, w_ref, sc_ref, y_ref, st_ref, ext_ref, a_ref, b_ref,
                 *, tm, group_w, halo_is_state, mxu_dtype):
    i = pl.program_id(1)
    x = x_ref[...]
    h = _rms(x, g0_ref[...])
    if halo_is_state:
        ext_ref[0:POOL_HALO - POOL_STATE, :] = jnp.zeros((POOL_HALO - POOL_STATE, x.shape[1]), F32)
        ext_ref[POOL_HALO - POOL_STATE:POOL_HALO, :] = halo_ref[...]
    else:
        ext_ref[0:POOL_HALO, :] = jnp.where(i > 0, _rms(halo_ref[...], g0_ref[...]), 0.0)
    ext_ref[POOL_HALO:POOL_HALO + tm, :] = h

    r_end = POOL_HALO + tm
    g1c, g2c, g3c = group_w, 2 * group_w, 3 * group_w
    a_ref[8:r_end, :] = ext_ref[8:r_end, :] + ext_ref[7:r_end - 1, :]
    b_ref[16:r_end, g1c:] = a_ref[16:r_end, g1c:] + a_ref[14:r_end - 2, g1c:]
    a_ref[24:r_end, g2c:] = b_ref[24:r_end, g2c:] + b_ref[20:r_end - 4, g2c:]
    sums = [a_ref[POOL_HALO:r_end, 0:g1c], b_ref[POOL_HALO:r_end, g1c:g2c],
            a_ref[POOL_HALO:r_end, g2c:g3c],
            a_ref[POOL_HALO:r_end, g3c:] + a_ref[POOL_HALO - 8:r_end - 8, g3c:]]

    t = i * tm + lax.broadcasted_iota(jnp.int32, (tm, 1), 0)
    ys = []
    for gi, win in enumerate(POOL_WINDOWS):
        cols = slice(gi * group_w, (gi + 1) * group_w)
        acc = sums[gi]
        if halo_is_state:
            mean = acc * (1.0 / win)
        else:
            mean = acc / jnp.minimum(t + 1, win).astype(F32)
        pooled = mean - h[:, cols]
        ys.append(jnp.dot(pooled.astype(mxu_dtype), w_ref[gi].astype(mxu_dtype),
                          preferred_element_type=F32))
    y = jnp.concatenate(ys, axis=1) * sc_ref[...]
    y_ref[...] = x + _rms(y, g1_ref[...])
    st_ref[...] = ext_ref[POOL_HALO + tm - POOL_STATE:POOL_HALO + tm, :]


def _pool(x3, state, g0, g1, w_pool, layer, scale, tm, mxu_dtype):
    b, t, d = x3.shape
    tm = _tile(t, tm, POOL_HALO)
    group_w = d // len(POOL_WINDOWS)
    halo_is_state = state is not None
    if halo_is_state:
        assert t == tm
        halo, halo_spec = state, pl.BlockSpec((None, POOL_STATE, d), lambda bi, i: (bi, 0, 0))
    else:
        per = tm // POOL_HALO
        halo = x3
        halo_spec = pl.BlockSpec((None, POOL_HALO, d), lambda bi, i: (bi, jnp.maximum(i * per - 1, 0), 0))
    vec = pl.BlockSpec((1, d), lambda bi, i: (0, 0))
    return pl.pallas_call(
        functools.partial(_pool_kernel, tm=tm, group_w=group_w, halo_is_state=halo_is_state,
                          mxu_dtype=mxu_dtype),
        out_shape=(jax.ShapeDtypeStruct((b, t, d), F32), jax.ShapeDtypeStruct((b, POOL_STATE, d), F32)),
        grid=(b, t // tm),
        in_specs=[pl.BlockSpec((None, tm, d), lambda bi, i: (bi, i, 0)), halo_spec, vec, vec,
                  pl.BlockSpec((None,) + w_pool.shape[1:], lambda bi, i: (layer, 0, 0, 0)), vec],
        out_specs=(pl.BlockSpec((None, tm, d), lambda bi, i: (bi, i, 0)),
                   pl.BlockSpec((None, POOL_STATE, d), lambda bi, i: (bi, 0, 0))),
        scratch_shapes=[pltpu.VMEM((POOL_HALO + tm, d), F32)] * 3,
        compiler_params=_params("parallel", "arbitrary"),
        name="pool",
    )(x3, halo, g0.reshape(1, d), g1.reshape(1, d), w_pool, scale.reshape(1, d))


def _trunk(x3, ret0, pool0, paged, weights, lg, slopes, *, row_tile, small, side_cache=None):
    w_in_bf, w_in, w_out, w_pool_bf, w_pool, pool_scale, w_up, w_down, mlp_w, norm_g = weights
    mlp_w = list(mlp_w)
    b, t, d = x3.shape
    depth = w_up.shape[0]
    n_ret = ret0.shape[2]
    n_even = ret0.shape[0]
    group_w = n_ret * HEAD_DIM
    x = x3.reshape(b * t, d)
    mxu_small = F32 if small else BF16
    new_k, new_v, new_ret, new_pool, ksum_parts = [], [], [], [], []
    for l in range(depth):
        if l % 2 == 0:
            e = l // 2
            z, k_new, v_new = _inproj(x, norm_g[l, 0], w_in if small else w_in_bf, e, group_w,
                                      2 * row_tile, 512)
            z3 = z.reshape(b, t, -1)
            k3 = k_new.reshape(b, t, group_w)
            v3 = v_new.reshape(b, t, group_w)
            o_r, s_new, cast_w = _retention(z3, ret0[e], lg, n_ret, 256, mxu_small,
                                            F32 if small else BF16,
                                            (w_up, w_down, l) if mlp_w[l] is None else None)
            if mlp_w[l] is None:
                if cast_w is None:
                    cast_w = (w_up[l].astype(BF16), w_down[l].astype(BF16))
                mlp_w[l] = (cast_w[0][None], cast_w[1][None], 0)
            new_k.append(k3.reshape(b, t, n_ret, HEAD_DIM))
            new_v.append(v3.reshape(b, t, n_ret, HEAD_DIM))
            if paged is None:
                o_b = _moba_prompt(z3, k3, v3, slopes, n_ret, 4)
            else:
                cache_k, cache_v, page_table, ksums = paged
                n_seq, n_pages = page_table.shape
                pt_flat = page_table.reshape(-1)
                sel = _moba_select(ksums[e], z3, 4)
                sel_flat = sel[..., :MOBA_TOPK].reshape(-1)
                o_b = _moba_sample(z3, k3, v3, cache_k, cache_v, e, pt_flat, sel_flat, slopes, n_pages, 4)
            x = _outproj(o_r.reshape(b * t, -1), o_b.reshape(b * t, -1), w_out, e, x, norm_g[l, 1],
                         row_tile)
            new_ret.append(s_new)
        else:
            o = l // 2
            wp = w_pool if small else w_pool_bf
            y3, p_new = _pool(x.reshape(b, t, d), None if pool0 is None else pool0[o],
                              norm_g[l, 0], norm_g[l, 1], wp, o, pool_scale[o], 256, mxu_small)
            x = y3.reshape(b * t, d)
            new_pool.append(p_new)
        side = None
        if side_cache is not None:
            cache_k, page_table = side_cache
            per_layer = depth // n_even
            n_side_pages = page_table.size // per_layer
            side = (cache_k, l // per_layer, page_table.reshape(-1), (l % per_layer) * n_side_pages,
                    n_side_pages)
        x = _mlp(x, norm_g[l, 2], norm_g[l, 3], *mlp_w[l], 2 * row_tile, 512, side)
        if side is not None:
            x, part = x
            ksum_parts.append(part)
    ksums = None
    if side_cache is not None:
        n_seq = side_cache[1].shape[0]
        per_layer = depth // n_even
        ksums = []
        for e in range(n_even):
            ks = jnp.concatenate(ksum_parts[e * per_layer:(e + 1) * per_layer], axis=0)
            ks = ks.reshape((n_seq, -1) + ks.shape[1:])
            ksums.append(ks.transpose(0, 2, 1, 3))
    return (x.reshape(b, t, d), jnp.stack(new_k), jnp.stack(new_v), jnp.stack(new_ret),
            jnp.stack(new_pool), ksums, mlp_w)


def kernel(x_prompt, x_sample, cache_k, cache_v, state_ret, state_pool, page_table, w_in, w_out, w_pool,
           pool_scale, w_up, w_down, norm_g):
    n_ret = state_ret.shape[2]
    n_moba = cache_k.shape[3]
    assert n_moba == n_ret and w_in.shape[2] == 7 * n_ret * HEAD_DIM
    lg = jnp.log1p(-jnp.exp2(-5.0 - jnp.arange(n_ret, dtype=F32)))
    slopes = jnp.exp2(-8.0 * (jnp.arange(n_moba, dtype=F32) + 1.0) / n_moba)
    depth = w_up.shape[0]
    w_up_odd, w_down_odd = w_up[1::2].astype(BF16), w_down[1::2].astype(BF16)
    mlp_w = [(w_up_odd, w_down_odd, l // 2) if l % 2 else None for l in range(depth)]
    shared = (w_in.astype(BF16), w_in, w_out.astype(BF16), w_pool.astype(BF16), w_pool, pool_scale,
              w_up, w_down)
    n_even = state_ret.shape[0]
    ret_zero = jnp.zeros((n_even, x_prompt.shape[0], n_ret, HEAD_DIM, HEAD_DIM), F32)

    y_p, k_p, v_p, ret_p, pool_p, ksums, mlp_w = _trunk(
        x_prompt, ret_zero, None, None, shared + (mlp_w, norm_g), lg, slopes,
        row_tile=512, small=False, side_cache=(cache_k, page_table))
    y_s, k_s, v_s, ret_s, pool_s, _, _ = _trunk(
        x_sample, state_ret, state_pool, (cache_k, cache_v, page_table, ksums),
        shared + (mlp_w, norm_g), lg, slopes, row_tile=512, small=True)
    return (y_p, y_s, k_p, v_p, k_s, v_s, ret_p, ret_s, pool_p, pool_s)
```

```python
import functools

import jax
import jax.numpy as jnp
from jax import lax
from jax.experimental import pallas as pl
from jax.experimental.pallas import tpu as pltpu

F32 = jnp.float32
BF16 = jnp.bfloat16

HEAD_DIM = 128
MOBA_BLOCK = 256
MOBA_TOPK = 3
POOL_WINDOWS = (2, 4, 8, 16)
POOL_HALO = 32
assert POOL_WINDOWS == (2, 4, 8, 16)
POOL_STATE = max(POOL_WINDOWS) - 1
EPS = 1e-6
NEG = -1e30
LOG2E = 1.4426950408889634
VMEM_LIMIT = 48 * 1024 * 1024
VMEM_LIMIT_BIG = 58 * 1024 * 1024

_NT = (((1,), (1,)), ((), ()))
_TN = (((0,), (0,)), ((), ()))


def _params(*sem, vmem=VMEM_LIMIT):
    return pltpu.CompilerParams(dimension_semantics=sem, vmem_limit_bytes=vmem)


def _tile(n, target, align):
    if n <= target:
        return n
    t = target - target % align
    while n % t:
        t -= align
    return t


def _rms(x, g):
    return x * lax.rsqrt(jnp.mean(x * x, axis=-1, keepdims=True) + EPS) * g


def _inproj_kernel(x_ref, g_ref, w_ref, z_ref, k_ref, v_ref, h_ref, *, n_main, n_kv):
    j = pl.program_id(1)

    @pl.when(j == 0)
    def _():
        h_ref[...] = _rms(x_ref[...], g_ref[...]).astype(BF16)

    def emit(o_ref):
        o_ref[...] = jnp.dot(h_ref[...], w_ref[...], preferred_element_type=F32)

    pl.when(j < n_main)(lambda: emit(z_ref))
    pl.when((j >= n_main) & (j < n_main + n_kv))(lambda: emit(k_ref))
    pl.when(j >= n_main + n_kv)(lambda: emit(v_ref))


def _inproj(x, g, w, layer, group_w, tm, tn):
    m, d = x.shape
    tm, tn = _tile(m, tm, 8), _tile(group_w, tn, 128)
    n_main, n_kv = 5 * group_w // tn, group_w // tn
    assert w.shape[2] == 7 * group_w
    return pl.pallas_call(
        functools.partial(_inproj_kernel, n_main=n_main, n_kv=n_kv),
        out_shape=(jax.ShapeDtypeStruct((m, 5 * group_w), F32),
                   jax.ShapeDtypeStruct((m, group_w), F32),
                   jax.ShapeDtypeStruct((m, group_w), F32)),
        grid=(m // tm, n_main + 2 * n_kv),
        in_specs=[pl.BlockSpec((tm, d), lambda i, j: (i, 0)),
                  pl.BlockSpec((1, d), lambda i, j: (0, 0)),
                  pl.BlockSpec((None, d, tn), lambda i, j: (layer, 0, j))],
        out_specs=(pl.BlockSpec((tm, tn), lambda i, j: (i, jnp.minimum(j, n_main - 1))),
                   pl.BlockSpec((tm, tn), lambda i, j: (i, jnp.clip(j - n_main, 0, n_kv - 1))),
                   pl.BlockSpec((tm, tn), lambda i, j: (i, jnp.clip(j - n_main - n_kv, 0, n_kv - 1)))),
        scratch_shapes=[pltpu.VMEM((tm, d), BF16)],
        compiler_params=_params("parallel", "arbitrary"),
        name="in_proj",
    )(x, g.reshape(1, d), w)


def _mlp_kernel(*refs, n_side, ppb):
    pages = refs[6:6 + n_side]
    x_ref, g2_ref, g3_ref, wu_ref, wd_ref = refs[1:6]
    if n_side:
        o_ref, ks_ref, h_ref = refs[6 + n_side:]
    else:
        o_ref, h_ref = refs[6:]
    f = pl.program_id(1)

    @pl.when(f == 0)
    def _():
        h_ref[...] = _rms(x_ref[...], g2_ref[...]).astype(BF16)
        o_ref[...] = jnp.zeros_like(o_ref)

    u = jnp.dot(h_ref[...], wu_ref[...], preferred_element_type=F32)
    a = jnp.square(jnp.maximum(u, 0.0)).astype(BF16)
    o_ref[...] += jnp.dot(a, wd_ref[...], preferred_element_type=F32)

    for blk in range(n_side // ppb):
        s = jnp.sum(pages[blk * ppb][...], axis=0)
        for r in range(1, ppb):
            s = s + jnp.sum(pages[blk * ppb + r][...], axis=0)
        ks_ref[blk] = s

    @pl.when(f == pl.num_programs(1) - 1)
    def _():
        o_ref[...] = x_ref[...] + _rms(o_ref[...], g3_ref[...])


def _mlp(x, g2, g3, w_up, w_down, layer, tm, tf, side=None):
    m, d = x.shape
    ff = w_up.shape[2]
    tm, tf = _tile(m, tm, 8), _tile(ff, tf, 128)
    nf = ff // tf
    steps = (m // tm) * nf
    in_specs = [pl.BlockSpec((tm, d), lambda i, f, pt: (i, 0)),
                pl.BlockSpec((1, d), lambda i, f, pt: (0, 0)),
                pl.BlockSpec((1, d), lambda i, f, pt: (0, 0)),
                pl.BlockSpec((None, d, tf), lambda i, f, pt: (layer, 0, f)),
                pl.BlockSpec((None, tf, d), lambda i, f, pt: (layer, f, 0))]
    out_shape = [jax.ShapeDtypeStruct((m, d), F32)]
    out_specs = [pl.BlockSpec((tm, d), lambda i, f, pt: (i, 0))]
    args = [x, g2.reshape(1, d), g3.reshape(1, d), w_up, w_down]
    n_side, ppb = 0, 1
    pt_flat = jnp.zeros((1,), jnp.int32)
    if side is not None:
        cache, cache_layer, pt_flat, first_page, n_side_pages = side
        _, _, page, n_heads, hd = cache.shape
        ppb = MOBA_BLOCK // page
        n_side = n_side_pages // steps
        assert n_side * steps == n_side_pages and n_side % ppb == 0 and first_page % ppb == 0

        def page_spec(r):
            return pl.BlockSpec(
                (None, None, page, n_heads, hd),
                lambda i, f, pt: (cache_layer, pt[first_page + (i * nf + f) * n_side + r], 0, 0, 0))

        in_specs += [page_spec(r) for r in range(n_side)]
        args += [cache] * n_side
        out_shape.append(jax.ShapeDtypeStruct((n_side_pages // ppb, n_heads, hd), F32))
        out_specs.append(pl.BlockSpec((n_side // ppb, n_heads, hd), lambda i, f, pt: (i * nf + f, 0, 0)))
    out = pl.pallas_call(
        functools.partial(_mlp_kernel, n_side=n_side, ppb=ppb),
        out_shape=tuple(out_shape),
        grid_spec=pltpu.PrefetchScalarGridSpec(
            num_scalar_prefetch=1,
            grid=(m // tm, nf),
            in_specs=in_specs,
            out_specs=tuple(out_specs),
            scratch_shapes=[pltpu.VMEM((tm, d), BF16)]),
        compiler_params=_params("parallel", "arbitrary", vmem=VMEM_LIMIT_BIG),
        name="mlp",
    )(pt_flat, *args)
    return out if side is not None else out[0]


def _outproj_kernel(or_ref, ob_ref, w_ref, x_ref, g_ref, o_ref, *, ret_w):
    m = jnp.dot(or_ref[...].astype(BF16), w_ref[0:ret_w, :], preferred_element_type=F32)
    m = m + jnp.dot(ob_ref[...].astype(BF16), w_ref[ret_w:, :], preferred_element_type=F32)
    o_ref[...] = x_ref[...] + _rms(m, g_ref[...])


def _outproj(o_r, o_b, w_out, layer, x, g, tm):
    m, d = x.shape
    ret_w, moba_w = o_r.shape[1], o_b.shape[1]
    tm = _tile(m, tm, 8)
    return pl.pallas_call(
        functools.partial(_outproj_kernel, ret_w=ret_w),
        out_shape=jax.ShapeDtypeStruct((m, d), F32),
        grid=(m // tm,),
        in_specs=[pl.BlockSpec((tm, ret_w), lambda i: (i, 0)),
                  pl.BlockSpec((tm, moba_w), lambda i: (i, 0)),
                  pl.BlockSpec((None, ret_w + moba_w, d), lambda i: (layer, 0, 0)),
                  pl.BlockSpec((tm, d), lambda i: (i, 0)),
                  pl.BlockSpec((1, d), lambda i: (0, 0))],
        out_specs=pl.BlockSpec((tm, d), lambda i: (i, 0)),
        compiler_params=_params("parallel"),
        name="outproj",
    )(o_r, o_b, w_out, x, g.reshape(1, d))


RET_HEADS_PER_STEP = 8


def _retention_kernel(*refs, chunk, hg, mxu_dtype, n_cast):
    lg_ref, q_ref, k_ref, v_ref, g_ref, s0_ref = refs[:6]
    cast_in = refs[6:6 + n_cast]
    o_ref, sfin_ref = refs[6 + n_cast:8 + n_cast]
    cast_out = refs[8 + n_cast:8 + 2 * n_cast]
    s_ref, decay_ref = refs[8 + 2 * n_cast:]
    for src, dst in zip(cast_in, cast_out):
        dst[...] = src[...].astype(dst.dtype)
    d = HEAD_DIM
    g = pl.program_id(1)
    c = pl.program_id(2)
    heads = range(hg)
    lgs = [lg_ref[g * hg + u] for u in heads]

    @pl.when(c == 0)
    def _():
        s_ref[...] = s0_ref[...]
        ii = lax.broadcasted_iota(jnp.int32, (chunk, chunk), 0)
        jj = lax.broadcasted_iota(jnp.int32, (chunk, chunk), 1)
        diff = (ii - jj).astype(F32)
        for u in heads:
            decay_ref[u] = jnp.where(diff >= 0, jnp.exp(lgs[u] * jnp.maximum(diff, 0.0)), 0.0)

    idx = lax.broadcasted_iota(jnp.int32, (chunk, 1), 0).astype(F32)
    cols = [slice(u * d, (u + 1) * d) for u in heads]
    ks = [k_ref[:, cols[u]] * (d ** -0.5) for u in heads]
    qm = [q_ref[:, cols[u]].astype(mxu_dtype) for u in heads]
    vm = [v_ref[:, cols[u]].astype(mxu_dtype) for u in heads]
    states = [s_ref[u] for u in heads]

    raw = [lax.dot_general(qm[u], ks[u].astype(mxu_dtype), _NT, preferred_element_type=F32)
           for u in heads]
    qs = [jnp.dot(qm[u], states[u].astype(mxu_dtype), preferred_element_type=F32) for u in heads]
    inner = [(raw[u] * decay_ref[u]).astype(mxu_dtype) for u in heads]
    kw = [(ks[u] * jnp.exp(lgs[u] * (chunk - 1.0 - idx))).astype(mxu_dtype) for u in heads]
    intra = [jnp.dot(inner[u], vm[u], preferred_element_type=F32) for u in heads]
    kv = [lax.dot_general(kw[u], vm[u], _TN, preferred_element_type=F32) for u in heads]

    last = c == pl.num_programs(2) - 1
    for u in heads:
        o = intra[u] + qs[u] * jnp.exp(lgs[u] * (idx + 1.0))
        s_new = jnp.exp(jnp.full((1, d), lgs[u] * chunk, F32)) * states[u] + kv[u]
        s_ref[u] = s_new
        mu = jnp.mean(o, axis=-1, keepdims=True)
        oc = o - mu
        var = jnp.mean(oc * oc, axis=-1, keepdims=True)
        gate = g_ref[:, cols[u]]
        silu = gate / (1.0 + jnp.exp(-gate))
        o_ref[:, cols[u]] = (oc * lax.rsqrt(var + EPS) * silu).astype(o_ref.dtype)

        @pl.when(last)
        def _():
            sfin_ref[u] = s_new


def _retention(z3, s0, lg, n_heads, chunk, mxu_dtype, out_dtype, cast=None):
    b, t, _ = z3.shape
    d = HEAD_DIM
    chunk = _tile(t, chunk, 8)
    hg = _tile(n_heads, RET_HEADS_PER_STEP, 1)
    n_groups = n_heads // hg
    nc = t // chunk
    steps = b * n_groups * nc

    def col(group):
        return pl.BlockSpec((None, chunk, hg * d), lambda bi, g, c: (bi, c, group * n_groups + g))

    state_spec = pl.BlockSpec((None, hg, d, d), lambda bi, g, c: (bi, g, 0, 0))
    in_specs = [pl.BlockSpec(memory_space=pltpu.SMEM), col(0), col(1), col(2), col(3), state_spec]
    out_shape = [jax.ShapeDtypeStruct((b, t, n_heads * d), out_dtype),
                 jax.ShapeDtypeStruct((b, n_heads, d, d), F32)]
    out_specs = [pl.BlockSpec((None, chunk, hg * d), lambda bi, g, c: (bi, c, g)), state_spec]
    args = [lg, z3, z3, z3, z3, s0]
    n_cast = 0
    if cast is not None and cast[0].shape[2] % (steps * 128) == 0:
        w_up, w_down, layers = cast
        _, dm, ff = w_up.shape
        tf = ff // steps
        n_cast = 2 * len(layers)

        def step(bi, g, c):
            return (bi * n_groups + g) * nc + c

        def up_spec(layer):
            return pl.BlockSpec((None, dm, tf), lambda bi, g, c: (layer, 0, step(bi, g, c)))

        def down_spec(layer):
            return pl.BlockSpec((None, tf, dm), lambda bi, g, c: (layer, step(bi, g, c), 0))

        for layer in layers:
            in_specs += [up_spec(layer), down_spec(layer)]
            out_shape += [jax.ShapeDtypeStruct((dm, ff), BF16), jax.ShapeDtypeStruct((ff, dm), BF16)]
            out_specs += [pl.BlockSpec((dm, tf), lambda bi, g, c: (0, step(bi, g, c))),
                          pl.BlockSpec((tf, dm), lambda bi, g, c: (step(bi, g, c), 0))]
            args += [w_up, w_down]
    out = pl.pallas_call(
        functools.partial(_retention_kernel, chunk=chunk, hg=hg, mxu_dtype=mxu_dtype, n_cast=n_cast),
        out_shape=tuple(out_shape),
        grid=(b, n_groups, nc),
        in_specs=in_specs,
        out_specs=tuple(out_specs),
        scratch_shapes=[pltpu.VMEM((hg, d, d), F32), pltpu.VMEM((hg, chunk, chunk), F32)],
        compiler_params=_params("parallel", "parallel", "arbitrary"),
        name="retention",
    )(*args)
    casts = [(out[2 + 2 * n], out[3 + 2 * n]) for n in range(n_cast // 2)]
    return out[0], out[1], casts or None


def _topk_rank(bs, n_blocks, n_past, axis):
    idx = lax.broadcasted_iota(jnp.int32, bs.shape, axis)
    cnt = jnp.zeros(bs.shape, jnp.int32)
    for n in range(n_blocks):
        sc = bs[n:n + 1, :] if axis == 0 else bs[:, n:n + 1]
        beats = jnp.where(sc > bs, 1, jnp.where((sc == bs) & (n < idx), 1, 0))
        cnt = cnt + jnp.where(n < n_past, beats, 0)
    return cnt, idx


MOBA_HEADS_PER_STEP = 8
SUM_ROWS = 16


def _moba_prompt_kernel(slopes_ref, q_ref, k_ref, v_ref, o_ref, vt_ref, kmean_ref, bias_ref,
                        alibi_ref, *, nb, nbp, hg):
    blk, d = MOBA_BLOCK, HEAD_DIM
    g = pl.program_id(1)
    i = pl.program_id(2)
    heads = range(hg)

    ones_row = jnp.where(lax.broadcasted_iota(jnp.int32, (SUM_ROWS, blk), 0) == 0, 1.0, 0.0).astype(BF16)

    @pl.when(i == 0)
    def _():
        kmean_ref[...] = jnp.zeros_like(kmean_ref)
        for u in heads:
            for n in range(nb):
                kblk = k_ref[n * blk:(n + 1) * blk, u * d:(u + 1) * d]
                kmean_ref[u, n:n + 1, :] = jnp.mean(kblk, axis=0, keepdims=True)
                vt_ref[u, n, 0:d, :] = v_ref[n * blk:(n + 1) * blk, u * d:(u + 1) * d].T.astype(BF16)
                vt_ref[u, n, d:, :] = ones_row

    def kblock(u, n):
        return k_ref[pl.ds(pl.multiple_of(n * blk, blk), blk), u * d:(u + 1) * d].astype(BF16)

    kk = lax.broadcasted_iota(jnp.int32, (blk, blk), 0)
    qq = lax.broadcasted_iota(jnp.int32, (blk, blk), 1)
    slopes = [slopes_ref[g * hg + u] * LOG2E for u in heads]
    qtbs, carry0 = [], []
    for u in heads:
        qt = q_ref[:, u * d:(u + 1) * d].T
        bs = jnp.dot(kmean_ref[u], qt, precision=lax.Precision.HIGHEST,
                     preferred_element_type=F32)
        cnt, bidx = _topk_rank(bs, nbp, i, 0)
        bias_ref[u] = jnp.where((bidx < i) & (cnt < MOBA_TOPK), 0.0, NEG)
        qtb = (qt * (d ** -0.5 * LOG2E)).astype(BF16)
        alibi = slopes[u] * kk.astype(F32)
        alibi_ref[u] = alibi
        s = jnp.dot(kblock(u, i), qtb, preferred_element_type=F32)
        s = jnp.where(kk <= qq, s + alibi, NEG)
        m0 = jnp.max(s, axis=0, keepdims=True)
        p = jnp.exp2(s - m0)
        acc0 = jnp.dot(vt_ref[u, i], p.astype(BF16), preferred_element_type=F32)
        qtbs.append(qtb)
        carry0 += [m0, acc0]

    def body(j, carry):
        shift = ((i - j) * blk).astype(F32)
        raw = [jnp.dot(kblock(u, j), qtbs[u], preferred_element_type=F32) for u in heads]
        stats, probs = [], []
        for u in heads:
            m = carry[2 * u]
            brow = bias_ref[u, pl.ds(j, 1), :] - slopes[u] * shift
            sj = raw[u] + alibi_ref[u]
            m_new = jnp.maximum(m, jnp.max(sj, axis=0, keepdims=True) + brow)
            stats.append((m_new, jnp.exp2(m - m_new)))
            probs.append(jnp.exp2(sj - (m_new - brow)).astype(BF16))
        out = []
        for u in heads:
            m_new, alpha = stats[u]
            pv = jnp.dot(vt_ref[u, j], probs[u], preferred_element_type=F32)
            out += [m_new, alpha * carry[2 * u + 1] + pv]
        return tuple(out)

    fin = lax.fori_loop(0, i, body, tuple(carry0))
    for u in heads:
        acc = fin[2 * u + 1]
        o_ref[:, u * d:(u + 1) * d] = (acc[0:d] / acc[d:d + 1]).T.astype(o_ref.dtype)


def _moba_prompt(z3, k3, v3, slopes, n_heads, q_group):
    b, t, _ = z3.shape
    blk, d = MOBA_BLOCK, HEAD_DIM
    nb = t // blk
    nbp = -(-nb // 8) * 8
    hg = _tile(n_heads, MOBA_HEADS_PER_STEP, 1)
    n_groups = n_heads // hg
    whole = pl.BlockSpec((None, t, hg * d), lambda bi, g, i: (bi, 0, g), pipeline_mode=pl.Buffered(1))
    return pl.pallas_call(
        functools.partial(_moba_prompt_kernel, nb=nb, nbp=nbp, hg=hg),
        out_shape=jax.ShapeDtypeStruct((b, t, n_heads * d), BF16),
        grid=(b, n_groups, nb),
        in_specs=[pl.BlockSpec(memory_space=pltpu.SMEM),
                  pl.BlockSpec((None, blk, hg * d), lambda bi, g, i: (bi, i, q_group * n_groups + g)),
                  whole, whole],
        out_specs=pl.BlockSpec((None, blk, hg * d), lambda bi, g, i: (bi, i, g)),
        scratch_shapes=[pltpu.VMEM((hg, nb, d + SUM_ROWS, blk), BF16),
                        pltpu.VMEM((hg, nbp, d), F32), pltpu.VMEM((hg, nbp, blk), F32),
                        pltpu.VMEM((hg, blk, blk), F32)],
        compiler_params=_params("parallel", "parallel", "arbitrary", vmem=VMEM_LIMIT_BIG),
        name="moba_prompt",
    )(slopes, z3, k3, v3)


def _moba_select_kernel(ks_ref, q_ref, o_ref, *, n_heads, nb):
    q = q_ref[...]
    lane = lax.broadcasted_iota(jnp.int32, (q.shape[0], HEAD_DIM), 1)
    for h in range(n_heads):
        kmean = ks_ref[h] * (1.0 / MOBA_BLOCK)
        bs = lax.dot_general(q[:, h * HEAD_DIM:(h + 1) * HEAD_DIM], kmean, _NT,
                             precision=lax.Precision.HIGHEST, preferred_element_type=F32)
        cnt, col = _topk_rank(bs, nb, nb, 1)
        out = jnp.zeros(lane.shape, jnp.int32)
        for r in range(MOBA_TOPK):
            idx = jnp.sum(jnp.where(cnt == r, col, 0).astype(F32), axis=1, keepdims=True)
            out = jnp.where(lane == r, idx.astype(jnp.int32), out)
        o_ref[h] = out


def _moba_select(ksum, z3, q_group):
    b, n_heads, nb, d = ksum.shape
    tq = z3.shape[1]
    assert nb >= MOBA_TOPK and tq <= MOBA_BLOCK
    return pl.pallas_call(
        functools.partial(_moba_select_kernel, n_heads=n_heads, nb=nb),
        out_shape=jax.ShapeDtypeStruct((b, n_heads, tq, HEAD_DIM), jnp.int32),
        grid=(b,),
        in_specs=[pl.BlockSpec((None, n_heads, nb, d), lambda bi: (bi, 0, 0, 0)),
                  pl.BlockSpec((None, tq, n_heads * d), lambda bi: (bi, 0, q_group))],
        out_specs=pl.BlockSpec((None, n_heads, tq, HEAD_DIM), lambda bi: (bi, 0, 0, 0)),
        compiler_params=_params("parallel"),
        name="moba_select",
    )(ksum, z3)


def _moba_sample_kernel(pt_ref, sel_ref, slopes_ref, q_ref, kn_ref, vn_ref, ck_ref, cv_ref, o_ref,
                        kbuf, vbuf, sem, *, layer, n_heads, n_pages, page, ppb, tq):
    b = pl.program_id(0)
    h = pl.program_id(1)
    step = b * n_heads + h
    n_steps = pl.num_programs(0) * n_heads
    npg = MOBA_TOPK * ppb
    slot = step % 2

    def copies(bi, hi, to_slot):
        out = []
        for t in range(tq):
            for n in range(npg):
                blk = sel_ref[((bi * n_heads + hi) * tq + t) * MOBA_TOPK + n // ppb]
                pg = pt_ref[bi * n_pages + blk * ppb + n % ppb]
                out.append(pltpu.make_async_copy(ck_ref.at[layer, pg, :, hi, :],
                                                 kbuf.at[to_slot, t * npg + n], sem.at[to_slot, 0]))
                out.append(pltpu.make_async_copy(cv_ref.at[layer, pg, :, hi, :],
                                                 vbuf.at[to_slot, t * npg + n], sem.at[to_slot, 1]))
        return out

    @pl.when(step == 0)
    def _():
        for cp in copies(b, h, slot):
            cp.start()

    @pl.when(step + 1 < n_steps)
    def _():
        nxt = step + 1
        for cp in copies(nxt // n_heads, nxt % n_heads, 1 - slot):
            cp.start()

    for cp in copies(b, h, slot):
        cp.wait()

    slope = slopes_ref[h]
    past_len = n_pages * page
    rowp = lax.broadcasted_iota(jnp.int32, (page, 1), 0)
    rowq = lax.broadcasted_iota(jnp.int32, (tq, 1), 0)
    kn = kn_ref[...]
    vn = vn_ref[...]

    def body(t, carry):
        q_t = q_ref[pl.ds(t, 1), :] * (HEAD_DIM ** -0.5)
        base = ((b * n_heads + h) * tq + t) * MOBA_TOPK
        scores = []
        for n in range(npg):
            kpos0 = sel_ref[base + n // ppb] * MOBA_BLOCK + (n % ppb) * page
            s = jnp.sum(kbuf[slot, t * npg + n] * q_t, axis=1, keepdims=True)
            dist = (past_len + t - kpos0 - rowp).astype(F32)
            scores.append(s - slope * dist)
        s_own = jnp.sum(kn * q_t, axis=1, keepdims=True)
        s_own = jnp.where(rowq <= t, s_own - slope * (t - rowq).astype(F32), NEG)
        m = jnp.max(s_own, axis=0, keepdims=True)
        for s in scores:
            m = jnp.maximum(m, jnp.max(s, axis=0, keepdims=True))
        p_own = jnp.exp(s_own - m)
        l = jnp.sum(p_own, axis=0, keepdims=True)
        acc = jnp.sum(p_own * vn, axis=0, keepdims=True)
        for n, s in enumerate(scores):
            p = jnp.exp(s - m)
            l = l + jnp.sum(p, axis=0, keepdims=True)
            acc = acc + jnp.sum(p * vbuf[slot, t * npg + n], axis=0, keepdims=True)
        o_ref[pl.ds(t, 1), :] = acc / l
        return carry

    lax.fori_loop(0, tq, body, 0)


def _moba_sample(z3, k3, v3, cache_k, cache_v, layer, pt_flat, sel_flat, slopes, n_pages, q_group):
    b, tq, _ = z3.shape
    _, _, page, n_heads, d = cache_k.shape
    ppb = MOBA_BLOCK // page
    n_bufs = tq * MOBA_TOPK * ppb
    new_rows = pl.BlockSpec((None, tq, d), lambda bi, h, pt, sel: (bi, 0, h))
    return pl.pallas_call(
        functools.partial(_moba_sample_kernel, layer=layer, n_heads=n_heads, n_pages=n_pages,
                          page=page, ppb=ppb, tq=tq),
        out_shape=jax.ShapeDtypeStruct((b, tq, n_heads * d), F32),
        grid_spec=pltpu.PrefetchScalarGridSpec(
            num_scalar_prefetch=2,
            grid=(b, n_heads),
            in_specs=[pl.BlockSpec(memory_space=pltpu.SMEM),
                      pl.BlockSpec((None, tq, d), lambda bi, h, pt, sel: (bi, 0, q_group * n_heads + h)),
                      new_rows, new_rows,
                      pl.BlockSpec(memory_space=pl.ANY), pl.BlockSpec(memory_space=pl.ANY)],
            out_specs=pl.BlockSpec((None, tq, d), lambda bi, h, pt, sel: (bi, 0, h)),
            scratch_shapes=[pltpu.VMEM((2, n_bufs, page, d), F32), pltpu.VMEM((2, n_bufs, page, d), F32),
                            pltpu.SemaphoreType.DMA((2, 2))]),
        compiler_params=_params("arbitrary", "arbitrary"),
        name="moba_sample",
    )(pt_flat, sel_flat, slopes, z3, k3, v3, cache_k, cache_v)


def _pool_kernel(x_ref, halo_ref, g0_ref, g1_ref, w_ref, sc_ref, y_ref, st_ref, ext_ref, a_ref, b_ref,
                 *, tm, group_w, halo_is_state, mxu_dtype):
    i = pl.program_id(1)
    x = x_ref[...]
    h = _rms(x, g0_ref[...])
    if halo_is_state:
        ext_ref[0:POOL_HALO - POOL_STATE, :] = jnp.zeros((POOL_HALO - POOL_STATE, x.shape[1]), F32)
        ext_ref[POOL_HALO - POOL_STATE:POOL_HALO, :] = halo_ref[...]
    else:
        ext_ref[0:POOL_HALO, :] = jnp.where(i > 0, _rms(halo_ref[...], g0_ref[...]), 0.0)
    ext_ref[POOL_HALO:POOL_HALO + tm, :] = h

    r_end = POOL_HALO + tm
    g1c, g2c, g3c = group_w, 2 * group_w, 3 * group_w
    a_ref[8:r_end, :] = ext_ref[8:r_end, :] + ext_ref[7:r_end - 1, :]
    b_ref[16:r_end, g1c:] = a_ref[16:r_end, g1c:] + a_ref[14:r_end - 2, g1c:]
    a_ref[24:r_end, g2c:] = b_ref[24:r_end, g2c:] + b_ref[20:r_end - 4, g2c:]
    sums = [a_ref[POOL_HALO:r_end, 0:g1c], b_ref[POOL_HALO:r_end, g1c:g2c],
            a_ref[POOL_HALO:r_end, g2c:g3c],
            a_ref[POOL_HALO:r_end, g3c:] + a_ref[POOL_HALO - 8:r_end - 8, g3c:]]

    t = i * tm + lax.broadcasted_iota(jnp.int32, (tm, 1), 0)
    ys = []
    for gi, win in enumerate(POOL_WINDOWS):
        cols = slice(gi * group_w, (gi + 1) * group_w)
        acc = sums[gi]
        if halo_is_state:
            mean = acc * (1.0 / win)
        else:
            mean = acc / jnp.minimum(t + 1, win).astype(F32)
        pooled = mean - h[:, cols]
        ys.append(jnp.dot(pooled.astype(mxu_dtype), w_ref[gi].astype(mxu_dtype),
                          preferred_element_type=F32))
    y = jnp.concatenate(ys, axis=1) * sc_ref[...]
    y_ref[...] = x + _rms(y, g1_ref[...])
    st_ref[...] = ext_ref[POOL_HALO + tm - POOL_STATE:POOL_HALO + tm, :]


def _pool(x3, state, g0, g1, w_pool, layer, scale, tm, mxu_dtype):
    b, t, d = x3.shape
    tm = _tile(t, tm, POOL_HALO)
    group_w = d // len(POOL_WINDOWS)
    halo_is_state = state is not None
    if halo_is_state:
        assert t == tm
        halo, halo_spec = state, pl.BlockSpec((None, POOL_STATE, d), lambda bi, i: (bi, 0, 0))
    else:
        per = tm // POOL_HALO
        halo = x3
        halo_spec = pl.BlockSpec((None, POOL_HALO, d), lambda bi, i: (bi, jnp.maximum(i * per - 1, 0), 0))
    vec = pl.BlockSpec((1, d), lambda bi, i: (0, 0))
    return pl.pallas_call(
        functools.partial(_pool_kernel, tm=tm, group_w=group_w, halo_is_state=halo_is_state,
                          mxu_dtype=mxu_dtype),
        out_shape=(jax.ShapeDtypeStruct((b, t, d), F32), jax.ShapeDtypeStruct((b, POOL_STATE, d), F32)),
        grid=(b, t // tm),
        in_specs=[pl.BlockSpec((None, tm, d), lambda bi, i: (bi, i, 0)), halo_spec, vec, vec,
                  pl.BlockSpec((None,) + w_pool.shape[1:], lambda bi, i: (layer, 0, 0, 0)), vec],
        out_specs=(pl.BlockSpec((None, tm, d), lambda bi, i: (bi, i, 0)),
                   pl.BlockSpec((None, POOL_STATE, d), lambda bi, i: (bi, 0, 0))),
        scratch_shapes=[pltpu.VMEM((POOL_HALO + tm, d), F32)] * 3,
        compiler_params=_params("parallel", "arbitrary"),
        name="pool",
    )(x3, halo, g0.reshape(1, d), g1.reshape(1, d), w_pool, scale.reshape(1, d))


def _trunk(x3, ret0, pool0, paged, weights, lg, slopes, *, row_tile, small, side_cache=None):
    w_in, w_out, w_pool_bf, w_pool, pool_scale, w_up, w_down, mlp_w, norm_g = weights
    mlp_w = list(mlp_w)
    b, t, d = x3.shape
    depth = w_up.shape[0]
    n_ret = ret0.shape[2]
    n_even = ret0.shape[0]
    group_w = n_ret * HEAD_DIM
    x = x3.reshape(b * t, d)
    mxu_small = F32 if small else BF16
    new_k, new_v, new_ret, new_pool, ksum_parts = [], [], [], [], []
    for l in range(depth):
        if l % 2 == 0:
            e = l // 2
            z, k_new, v_new = _inproj(x, norm_g[l, 0], w_in, e, group_w, 2 * row_tile, 512)
            z3 = z.reshape(b, t, -1)
            k3 = k_new.reshape(b, t, group_w)
            v3 = v_new.reshape(b, t, group_w)
            todo = [n for n in (l, l + 1) if n < depth and mlp_w[n] is None]
            o_r, s_new, casts = _retention(z3, ret0[e], lg, n_ret, 256, mxu_small,
                                           F32 if small else BF16,
                                           (w_up, w_down, todo) if todo else None)
            if todo and casts is None:
                casts = [(w_up[n].astype(BF16), w_down[n].astype(BF16)) for n in todo]
            for n, (wu, wd) in zip(todo, casts or []):
                mlp_w[n] = (wu[None], wd[None], 0)
            new_k.append(k3.reshape(b, t, n_ret, HEAD_DIM))
            new_v.append(v3.reshape(b, t, n_ret, HEAD_DIM))
            if paged is None:
                o_b = _moba_prompt(z3, k3, v3, slopes, n_ret, 4)
            else:
                cache_k, cache_v, page_table, ksums = paged
                n_seq, n_pages = page_table.shape
                pt_flat = page_table.reshape(-1)
                sel = _moba_select(ksums[e], z3, 4)
                sel_flat = sel[..., :MOBA_TOPK].reshape(-1)
                o_b = _moba_sample(z3, k3, v3, cache_k, cache_v, e, pt_flat, sel_flat, slopes, n_pages, 4)
            x = _outproj(o_r.reshape(b * t, -1), o_b.reshape(b * t, -1), w_out, e, x, norm_g[l, 1],
                         row_tile)
            new_ret.append(s_new)
        else:
            o = l // 2
            wp = w_pool if small else w_pool_bf
            y3, p_new = _pool(x.reshape(b, t, d), None if pool0 is None else pool0[o],
                              norm_g[l, 0], norm_g[l, 1], wp, o, pool_scale[o], 256, mxu_small)
            x = y3.reshape(b * t, d)
            new_pool.append(p_new)
        side = None
        if side_cache is not None:
            cache_k, page_table = side_cache
            per_layer = depth // n_even
            n_side_pages = page_table.size // per_layer
            side = (cache_k, l // per_layer, page_table.reshape(-1), (l % per_layer) * n_side_pages,
                    n_side_pages)
        x = _mlp(x, norm_g[l, 2], norm_g[l, 3], *mlp_w[l], 2 * row_tile, 512, side)
        if side is not None:
            x, part = x
            ksum_parts.append(part)
    ksums = None
    if side_cache is not None:
        n_seq = side_cache[1].shape[0]
        per_layer = depth // n_even
        ksums = []
        for e in range(n_even):
            ks = jnp.concatenate(ksum_parts[e * per_layer:(e + 1) * per_layer], axis=0)
            ks = ks.reshape((n_seq, -1) + ks.shape[1:])
            ksums.append(ks.transpose(0, 2, 1, 3))
    return (x.reshape(b, t, d), jnp.stack(new_k), jnp.stack(new_v), jnp.stack(new_ret),
            jnp.stack(new_pool), ksums, mlp_w)


def kernel(x_prompt, x_sample, cache_k, cache_v, state_ret, state_pool, page_table, w_in, w_out, w_pool,
           pool_scale, w_up, w_down, norm_g):
    n_ret = state_ret.shape[2]
    n_moba = cache_k.shape[3]
    assert n_moba == n_ret and w_in.shape[2] == 7 * n_ret * HEAD_DIM
    lg = jnp.log1p(-jnp.exp2(-5.0 - jnp.arange(n_ret, dtype=F32)))
    slopes = jnp.exp2(-8.0 * (jnp.arange(n_moba, dtype=F32) + 1.0) / n_moba)
    mlp_w = [None] * w_up.shape[0]
    shared = (w_in.astype(BF16), w_out.astype(BF16), w_pool.astype(BF16), w_pool, pool_scale,
              w_up, w_down)
    n_even = state_ret.shape[0]
    ret_zero = jnp.zeros((n_even, x_prompt.shape[0], n_ret, HEAD_DIM, HEAD_DIM), F32)

    y_p, k_p, v_p, ret_p, pool_p, ksums, mlp_w = _trunk(
        x_prompt, ret_zero, None, None, shared + (mlp_w, norm_g), lg, slopes,
        row_tile=512, small=False, side_cache=(cache_k, page_table))
    y_s, k_s, v_s, ret_s, pool_s, _, _ = _trunk(
        x_sample, state_ret, state_pool, (cache_k, cache_v, page_table, ksums),
        shared + (mlp_w, norm_g), lg, slopes, row_tile=512, small=True)
    return (y_p, y_s, k_p, v_p, k_s, v_s, ret_p, ret_s, pool_p, pool_s)
```

```python
import functools

import jax
import jax.numpy as jnp
from jax import lax
from jax.experimental import pallas as pl
from jax.experimental.pallas import tpu as pltpu

F32 = jnp.float32
BF16 = jnp.bfloat16

HEAD_DIM = 128
MOBA_BLOCK = 256
MOBA_TOPK = 3
POOL_WINDOWS = (2, 4, 8, 16)
POOL_HALO = 32
assert POOL_WINDOWS == (2, 4, 8, 16)
POOL_STATE = max(POOL_WINDOWS) - 1
EPS = 1e-6
NEG = -1e30
LOG2E = 1.4426950408889634
VMEM_LIMIT = 48 * 1024 * 1024
VMEM_LIMIT_BIG = 58 * 1024 * 1024

_NT = (((1,), (1,)), ((), ()))
_TN = (((0,), (0,)), ((), ()))


def _params(*sem, vmem=VMEM_LIMIT):
    return pltpu.CompilerParams(dimension_semantics=sem, vmem_limit_bytes=vmem)


def _tile(n, target, align):
    if n <= target:
        return n
    t = target - target % align
    while n % t:
        t -= align
    return t


def _rms(x, g):
    return x * lax.rsqrt(jnp.mean(x * x, axis=-1, keepdims=True) + EPS) * g


def _inproj_kernel(x_ref, g_ref, w_ref, z_ref, k_ref, v_ref, h_ref, *, n_main, n_kv):
    j = pl.program_id(1)

    @pl.when(j == 0)
    def _():
        h_ref[...] = _rms(x_ref[...], g_ref[...]).astype(BF16)

    def emit(o_ref):
        o_ref[...] = jnp.dot(h_ref[...], w_ref[...], preferred_element_type=F32)

    pl.when(j < n_main)(lambda: emit(z_ref))
    pl.when((j >= n_main) & (j < n_main + n_kv))(lambda: emit(k_ref))
    pl.when(j >= n_main + n_kv)(lambda: emit(v_ref))


def _inproj(x, g, w, layer, group_w, tm, tn):
    m, d = x.shape
    tm, tn = _tile(m, tm, 8), _tile(group_w, tn, 128)
    n_main, n_kv = 5 * group_w // tn, group_w // tn
    assert w.shape[2] == 7 * group_w
    return pl.pallas_call(
        functools.partial(_inproj_kernel, n_main=n_main, n_kv=n_kv),
        out_shape=(jax.ShapeDtypeStruct((m, 5 * group_w), F32),
                   jax.ShapeDtypeStruct((m, group_w), F32),
                   jax.ShapeDtypeStruct((m, group_w), F32)),
        grid=(m // tm, n_main + 2 * n_kv),
        in_specs=[pl.BlockSpec((tm, d), lambda i, j: (i, 0)),
                  pl.BlockSpec((1, d), lambda i, j: (0, 0)),
                  pl.BlockSpec((None, d, tn), lambda i, j: (layer, 0, j))],
        out_specs=(pl.BlockSpec((tm, tn), lambda i, j: (i, jnp.minimum(j, n_main - 1))),
                   pl.BlockSpec((tm, tn), lambda i, j: (i, jnp.clip(j - n_main, 0, n_kv - 1))),
                   pl.BlockSpec((tm, tn), lambda i, j: (i, jnp.clip(j - n_main - n_kv, 0, n_kv - 1)))),
        scratch_shapes=[pltpu.VMEM((tm, d), BF16)],
        compiler_params=_params("parallel", "arbitrary", vmem=VMEM_LIMIT_BIG),
        name="in_proj",
    )(x, g.reshape(1, d), w)


def _mlp_kernel(*refs, n_side, ppb):
    pages = refs[6:6 + n_side]
    x_ref, g2_ref, g3_ref, wu_ref, wd_ref = refs[1:6]
    if n_side:
        o_ref, ks_ref, h_ref = refs[6 + n_side:]
    else:
        o_ref, h_ref = refs[6:]
    f = pl.program_id(1)

    @pl.when(f == 0)
    def _():
        h_ref[...] = _rms(x_ref[...], g2_ref[...]).astype(BF16)
        o_ref[...] = jnp.zeros_like(o_ref)

    u = jnp.dot(h_ref[...], wu_ref[...], preferred_element_type=F32)
    a = jnp.square(jnp.maximum(u, 0.0)).astype(BF16)
    o_ref[...] += jnp.dot(a, wd_ref[...], preferred_element_type=F32)

    for blk in range(n_side // ppb):
        s = jnp.sum(pages[blk * ppb][...], axis=0)
        for r in range(1, ppb):
            s = s + jnp.sum(pages[blk * ppb + r][...], axis=0)
        ks_ref[blk] = s

    @pl.when(f == pl.num_programs(1) - 1)
    def _():
        o_ref[...] = x_ref[...] + _rms(o_ref[...], g3_ref[...])


def _mlp(x, g2, g3, w_up, w_down, layer, tm, tf, side=None):
    m, d = x.shape
    ff = w_up.shape[2]
    tm, tf = _tile(m, tm, 8), _tile(ff, tf, 128)
    nf = ff // tf
    steps = (m // tm) * nf
    in_specs = [pl.BlockSpec((tm, d), lambda i, f, pt: (i, 0)),
                pl.BlockSpec((1, d), lambda i, f, pt: (0, 0)),
                pl.BlockSpec((1, d), lambda i, f, pt: (0, 0)),
                pl.BlockSpec((None, d, tf), lambda i, f, pt: (layer, 0, f)),
                pl.BlockSpec((None, tf, d), lambda i, f, pt: (layer, f, 0))]
    out_shape = [jax.ShapeDtypeStruct((m, d), F32)]
    out_specs = [pl.BlockSpec((tm, d), lambda i, f, pt: (i, 0))]
    args = [x, g2.reshape(1, d), g3.reshape(1, d), w_up, w_down]
    n_side, ppb = 0, 1
    pt_flat = jnp.zeros((1,), jnp.int32)
    if side is not None:
        cache, cache_layer, pt_flat, first_page, n_side_pages = side
        _, _, page, n_heads, hd = cache.shape
        ppb = MOBA_BLOCK // page
        n_side = n_side_pages // steps
        assert n_side * steps == n_side_pages and n_side % ppb == 0 and first_page % ppb == 0

        def page_spec(r):
            return pl.BlockSpec(
                (None, None, page, n_heads, hd),
                lambda i, f, pt: (cache_layer, pt[first_page + (i * nf + f) * n_side + r], 0, 0, 0))

        in_specs += [page_spec(r) for r in range(n_side)]
        args += [cache] * n_side
        out_shape.append(jax.ShapeDtypeStruct((n_side_pages // ppb, n_heads, hd), F32))
        out_specs.append(pl.BlockSpec((n_side // ppb, n_heads, hd), lambda i, f, pt: (i * nf + f, 0, 0)))
    out = pl.pallas_call(
        functools.partial(_mlp_kernel, n_side=n_side, ppb=ppb),
        out_shape=tuple(out_shape),
        grid_spec=pltpu.PrefetchScalarGridSpec(
            num_scalar_prefetch=1,
            grid=(m // tm, nf),
            in_specs=in_specs,
            out_specs=tuple(out_specs),
            scratch_shapes=[pltpu.VMEM((tm, d), BF16)]),
        compiler_params=_params("parallel", "arbitrary", vmem=VMEM_LIMIT_BIG),
        name="mlp",
    )(pt_flat, *args)
    return out if side is not None else out[0]


def _outproj_kernel(or_ref, ob_ref, w_ref, x_ref, g_ref, o_ref, *, ret_w):
    m = jnp.dot(or_ref[...].astype(BF16), w_ref[0:ret_w, :], preferred_element_type=F32)
    m = m + jnp.dot(ob_ref[...].astype(BF16), w_ref[ret_w:, :], preferred_element_type=F32)
    o_ref[...] = x_ref[...] + _rms(m, g_ref[...])


def _outproj(o_r, o_b, w_out, layer, x, g, tm):
    m, d = x.shape
    ret_w, moba_w = o_r.shape[1], o_b.shape[1]
    tm = _tile(m, tm, 8)
    return pl.pallas_call(
        functools.partial(_outproj_kernel, ret_w=ret_w),
        out_shape=jax.ShapeDtypeStruct((m, d), F32),
        grid=(m // tm,),
        in_specs=[pl.BlockSpec((tm, ret_w), lambda i: (i, 0)),
                  pl.BlockSpec((tm, moba_w), lambda i: (i, 0)),
                  pl.BlockSpec((None, ret_w + moba_w, d), lambda i: (layer, 0, 0)),
                  pl.BlockSpec((tm, d), lambda i: (i, 0)),
                  pl.BlockSpec((1, d), lambda i: (0, 0))],
        out_specs=pl.BlockSpec((tm, d), lambda i: (i, 0)),
        compiler_params=_params("parallel"),
        name="outproj",
    )(o_r, o_b, w_out, x, g.reshape(1, d))


RET_HEADS_PER_STEP = 8


def _retention_kernel(*refs, chunk, hg, mxu_dtype, n_cast):
    lg_ref, q_ref, k_ref, v_ref, g_ref, s0_ref = refs[:6]
    cast_in = refs[6:6 + n_cast]
    o_ref, sfin_ref = refs[6 + n_cast:8 + n_cast]
    cast_out = refs[8 + n_cast:8 + 2 * n_cast]
    s_ref, decay_ref = refs[8 + 2 * n_cast:]
    for src, dst in zip(cast_in, cast_out):
        dst[...] = src[...].astype(dst.dtype)
    d = HEAD_DIM
    g = pl.program_id(1)
    c = pl.program_id(2)
    heads = range(hg)
    lgs = [lg_ref[g * hg + u] for u in heads]

    @pl.when(c == 0)
    def _():
        s_ref[...] = s0_ref[...]
        ii = lax.broadcasted_iota(jnp.int32, (chunk, chunk), 0)
        jj = lax.broadcasted_iota(jnp.int32, (chunk, chunk), 1)
        diff = (ii - jj).astype(F32)
        for u in heads:
            decay_ref[u] = jnp.where(diff >= 0, jnp.exp(lgs[u] * jnp.maximum(diff, 0.0)), 0.0)

    idx = lax.broadcasted_iota(jnp.int32, (chunk, 1), 0).astype(F32)
    cols = [slice(u * d, (u + 1) * d) for u in heads]
    ks = [k_ref[:, cols[u]] * (d ** -0.5) for u in heads]
    qm = [q_ref[:, cols[u]].astype(mxu_dtype) for u in heads]
    vm = [v_ref[:, cols[u]].astype(mxu_dtype) for u in heads]
    states = [s_ref[u] for u in heads]

    raw = [lax.dot_general(qm[u], ks[u].astype(mxu_dtype), _NT, preferred_element_type=F32)
           for u in heads]
    qs = [jnp.dot(qm[u], states[u].astype(mxu_dtype), preferred_element_type=F32) for u in heads]
    inner = [(raw[u] * decay_ref[u]).astype(mxu_dtype) for u in heads]
    kw = [(ks[u] * jnp.exp(lgs[u] * (chunk - 1.0 - idx))).astype(mxu_dtype) for u in heads]
    intra = [jnp.dot(inner[u], vm[u], preferred_element_type=F32) for u in heads]
    kv = [lax.dot_general(kw[u], vm[u], _TN, preferred_element_type=F32) for u in heads]

    last = c == pl.num_programs(2) - 1
    for u in heads:
        o = intra[u] + qs[u] * jnp.exp(lgs[u] * (idx + 1.0))
        s_new = jnp.exp(jnp.full((1, d), lgs[u] * chunk, F32)) * states[u] + kv[u]
        s_ref[u] = s_new
        mu = jnp.mean(o, axis=-1, keepdims=True)
        oc = o - mu
        var = jnp.mean(oc * oc, axis=-1, keepdims=True)
        gate = g_ref[:, cols[u]]
        silu = gate / (1.0 + jnp.exp(-gate))
        o_ref[:, cols[u]] = (oc * lax.rsqrt(var + EPS) * silu).astype(o_ref.dtype)

        @pl.when(last)
        def _():
            sfin_ref[u] = s_new


def _retention(z3, s0, lg, n_heads, chunk, mxu_dtype, out_dtype, cast=None):
    b, t, _ = z3.shape
    d = HEAD_DIM
    chunk = _tile(t, chunk, 8)
    hg = _tile(n_heads, RET_HEADS_PER_STEP, 1)
    n_groups = n_heads // hg
    nc = t // chunk
    steps = b * n_groups * nc

    def col(group):
        return pl.BlockSpec((None, chunk, hg * d), lambda bi, g, c: (bi, c, group * n_groups + g))

    state_spec = pl.BlockSpec((None, hg, d, d), lambda bi, g, c: (bi, g, 0, 0))
    in_specs = [pl.BlockSpec(memory_space=pltpu.SMEM), col(0), col(1), col(2), col(3), state_spec]
    out_shape = [jax.ShapeDtypeStruct((b, t, n_heads * d), out_dtype),
                 jax.ShapeDtypeStruct((b, n_heads, d, d), F32)]
    out_specs = [pl.BlockSpec((None, chunk, hg * d), lambda bi, g, c: (bi, c, g)), state_spec]
    args = [lg, z3, z3, z3, z3, s0]
    n_cast = 0
    if cast is not None and cast[0].shape[2] % (steps * 128) == 0:
        w_up, w_down, layers = cast
        _, dm, ff = w_up.shape
        tf = ff // steps
        n_cast = 2 * len(layers)

        def step(bi, g, c):
            return (bi * n_groups + g) * nc + c

        def up_spec(layer):
            return pl.BlockSpec((None, dm, tf), lambda bi, g, c: (layer, 0, step(bi, g, c)))

        def down_spec(layer):
            return pl.BlockSpec((None, tf, dm), lambda bi, g, c: (layer, step(bi, g, c), 0))

        for layer in layers:
            in_specs += [up_spec(layer), down_spec(layer)]
            out_shape += [jax.ShapeDtypeStruct((dm, ff), BF16), jax.ShapeDtypeStruct((ff, dm), BF16)]
            out_specs += [pl.BlockSpec((dm, tf), lambda bi, g, c: (0, step(bi, g, c))),
                          pl.BlockSpec((tf, dm), lambda bi, g, c: (step(bi, g, c), 0))]
            args += [w_up, w_down]
    out = pl.pallas_call(
        functools.partial(_retention_kernel, chunk=chunk, hg=hg, mxu_dtype=mxu_dtype, n_cast=n_cast),
        out_shape=tuple(out_shape),
        grid=(b, n_groups, nc),
        in_specs=in_specs,
        out_specs=tuple(out_specs),
        scratch_shapes=[pltpu.VMEM((hg, d, d), F32), pltpu.VMEM((hg, chunk, chunk), F32)],
        compiler_params=_params("parallel", "parallel", "arbitrary"),
        name="retention",
    )(*args)
    casts = [(out[2 + 2 * n], out[3 + 2 * n]) for n in range(n_cast // 2)]
    return out[0], out[1], casts or None


def _topk_rank(bs, n_blocks, n_past, axis):
    idx = lax.broadcasted_iota(jnp.int32, bs.shape, axis)
    cnt = jnp.zeros(bs.shape, jnp.int32)
    for n in range(n_blocks):
        sc = bs[n:n + 1, :] if axis == 0 else bs[:, n:n + 1]
        beats = jnp.where(sc > bs, 1, jnp.where((sc == bs) & (n < idx), 1, 0))
        cnt = cnt + jnp.where(n < n_past, beats, 0)
    return cnt, idx


MOBA_HEADS_PER_STEP = 8
SUM_ROWS = 16


def _moba_prompt_kernel(slopes_ref, q_ref, k_ref, v_ref, o_ref, vt_ref, kmean_ref, bias_ref,
                        alibi_ref, *, nb, nbp, hg):
    blk, d = MOBA_BLOCK, HEAD_DIM
    g = pl.program_id(1)
    i = pl.program_id(2)
    heads = range(hg)

    ones_row = jnp.where(lax.broadcasted_iota(jnp.int32, (SUM_ROWS, blk), 0) == 0, 1.0, 0.0).astype(BF16)

    @pl.when(i == 0)
    def _():
        kmean_ref[...] = jnp.zeros_like(kmean_ref)
        for u in heads:
            for n in range(nb):
                kblk = k_ref[n * blk:(n + 1) * blk, u * d:(u + 1) * d]
                kmean_ref[u, n:n + 1, :] = jnp.mean(kblk, axis=0, keepdims=True)
                vt_ref[u, n, 0:d, :] = v_ref[n * blk:(n + 1) * blk, u * d:(u + 1) * d].T.astype(BF16)
                vt_ref[u, n, d:, :] = ones_row

    def kblock(u, n):
        return k_ref[pl.ds(pl.multiple_of(n * blk, blk), blk), u * d:(u + 1) * d].astype(BF16)

    kk = lax.broadcasted_iota(jnp.int32, (blk, blk), 0)
    qq = lax.broadcasted_iota(jnp.int32, (blk, blk), 1)
    slopes = [slopes_ref[g * hg + u] * LOG2E for u in heads]
    qtbs, carry0 = [], []
    for u in heads:
        qt = q_ref[:, u * d:(u + 1) * d].T
        bs = jnp.dot(kmean_ref[u], qt, precision=lax.Precision.HIGHEST,
                     preferred_element_type=F32)
        cnt, bidx = _topk_rank(bs, nbp, i, 0)
        bias_ref[u] = jnp.where((bidx < i) & (cnt < MOBA_TOPK), 0.0, NEG)
        qtb = (qt * (d ** -0.5 * LOG2E)).astype(BF16)
        alibi = slopes[u] * kk.astype(F32)
        alibi_ref[u] = alibi
        s = jnp.dot(kblock(u, i), qtb, preferred_element_type=F32)
        s = jnp.where(kk <= qq, s + alibi, NEG)
        m0 = jnp.max(s, axis=0, keepdims=True)
        p = jnp.exp2(s - m0)
        acc0 = jnp.dot(vt_ref[u, i], p.astype(BF16), preferred_element_type=F32)
        qtbs.append(qtb)
        carry0 += [m0, acc0]

    def body(j, carry):
        shift = ((i - j) * blk).astype(F32)
        raw = [jnp.dot(kblock(u, j), qtbs[u], preferred_element_type=F32) for u in heads]
        stats, probs = [], []
        for u in heads:
            m = carry[2 * u]
            brow = bias_ref[u, pl.ds(j, 1), :] - slopes[u] * shift
            sj = raw[u] + alibi_ref[u]
            m_new = jnp.maximum(m, jnp.max(sj, axis=0, keepdims=True) + brow)
            stats.append((m_new, jnp.exp2(m - m_new)))
            probs.append(jnp.exp2(sj - (m_new - brow)).astype(BF16))
        out = []
        for u in heads:
            m_new, alpha = stats[u]
            pv = jnp.dot(vt_ref[u, j], probs[u], preferred_element_type=F32)
            out += [m_new, alpha * carry[2 * u + 1] + pv]
        return tuple(out)

    fin = lax.fori_loop(0, i, body, tuple(carry0))
    for u in heads:
        acc = fin[2 * u + 1]
        o_ref[:, u * d:(u + 1) * d] = (acc[0:d] / acc[d:d + 1]).T.astype(o_ref.dtype)


def _moba_prompt(z3, k3, v3, slopes, n_heads, q_group):
    b, t, _ = z3.shape
    blk, d = MOBA_BLOCK, HEAD_DIM
    nb = t // blk
    nbp = -(-nb // 8) * 8
    hg = _tile(n_heads, MOBA_HEADS_PER_STEP, 1)
    n_groups = n_heads // hg
    whole = pl.BlockSpec((None, t, hg * d), lambda bi, g, i: (bi, 0, g), pipeline_mode=pl.Buffered(1))
    return pl.pallas_call(
        functools.partial(_moba_prompt_kernel, nb=nb, nbp=nbp, hg=hg),
        out_shape=jax.ShapeDtypeStruct((b, t, n_heads * d), BF16),
        grid=(b, n_groups, nb),
        in_specs=[pl.BlockSpec(memory_space=pltpu.SMEM),
                  pl.BlockSpec((None, blk, hg * d), lambda bi, g, i: (bi, i, q_group * n_groups + g)),
                  whole, whole],
        out_specs=pl.BlockSpec((None, blk, hg * d), lambda bi, g, i: (bi, i, g)),
        scratch_shapes=[pltpu.VMEM((hg, nb, d + SUM_ROWS, blk), BF16),
                        pltpu.VMEM((hg, nbp, d), F32), pltpu.VMEM((hg, nbp, blk), F32),
                        pltpu.VMEM((hg, blk, blk), F32)],
        compiler_params=_params("parallel", "parallel", "arbitrary", vmem=VMEM_LIMIT_BIG),
        name="moba_prompt",
    )(slopes, z3, k3, v3)


def _moba_select_kernel(ks_ref, q_ref, o_ref, *, n_heads, nb):
    q = q_ref[...]
    lane = lax.broadcasted_iota(jnp.int32, (q.shape[0], HEAD_DIM), 1)
    for h in range(n_heads):
        kmean = ks_ref[h] * (1.0 / MOBA_BLOCK)
        bs = lax.dot_general(q[:, h * HEAD_DIM:(h + 1) * HEAD_DIM], kmean, _NT,
                             precision=lax.Precision.HIGHEST, preferred_element_type=F32)
        cnt, col = _topk_rank(bs, nb, nb, 1)
        out = jnp.zeros(lane.shape, jnp.int32)
        for r in range(MOBA_TOPK):
            idx = jnp.sum(jnp.where(cnt == r, col, 0).astype(F32), axis=1, keepdims=True)
            out = jnp.where(lane == r, idx.astype(jnp.int32), out)
        o_ref[h] = out


def _moba_select(ksum, z3, q_group):
    b, n_heads, nb, d = ksum.shape
    tq = z3.shape[1]
    assert nb >= MOBA_TOPK and tq <= MOBA_BLOCK
    return pl.pallas_call(
        functools.partial(_moba_select_kernel, n_heads=n_heads, nb=nb),
        out_shape=jax.ShapeDtypeStruct((b, n_heads, tq, HEAD_DIM), jnp.int32),
        grid=(b,),
        in_specs=[pl.BlockSpec((None, n_heads, nb, d), lambda bi: (bi, 0, 0, 0)),
                  pl.BlockSpec((None, tq, n_heads * d), lambda bi: (bi, 0, q_group))],
        out_specs=pl.BlockSpec((None, n_heads, tq, HEAD_DIM), lambda bi: (bi, 0, 0, 0)),
        compiler_params=_params("parallel"),
        name="moba_select",
    )(ksum, z3)


def _moba_sample_kernel(pt_ref, sel_ref, slopes_ref, q_ref, kn_ref, vn_ref, ck_ref, cv_ref, o_ref,
                        kbuf, vbuf, sem, *, layer, n_heads, n_pages, page, ppb, tq):
    b = pl.program_id(0)
    h = pl.program_id(1)
    step = b * n_heads + h
    n_steps = pl.num_programs(0) * n_heads
    npg = MOBA_TOPK * ppb
    slot = step % 2

    def copies(bi, hi, to_slot):
        out = []
        for t in range(tq):
            for n in range(npg):
                blk = sel_ref[((bi * n_heads + hi) * tq + t) * MOBA_TOPK + n // ppb]
                pg = pt_ref[bi * n_pages + blk * ppb + n % ppb]
                out.append(pltpu.make_async_copy(ck_ref.at[layer, pg, :, hi, :],
                                                 kbuf.at[to_slot, t * npg + n], sem.at[to_slot, 0]))
                out.append(pltpu.make_async_copy(cv_ref.at[layer, pg, :, hi, :],
                                                 vbuf.at[to_slot, t * npg + n], sem.at[to_slot, 1]))
        return out

    @pl.when(step == 0)
    def _():
        for cp in copies(b, h, slot):
            cp.start()

    @pl.when(step + 1 < n_steps)
    def _():
        nxt = step + 1
        for cp in copies(nxt // n_heads, nxt % n_heads, 1 - slot):
            cp.start()

    for cp in copies(b, h, slot):
        cp.wait()

    slope = slopes_ref[h]
    past_len = n_pages * page
    rowp = lax.broadcasted_iota(jnp.int32, (page, 1), 0)
    rowq = lax.broadcasted_iota(jnp.int32, (tq, 1), 0)
    kn = kn_ref[...]
    vn = vn_ref[...]

    def body(t, carry):
        q_t = q_ref[pl.ds(t, 1), :] * (HEAD_DIM ** -0.5)
        base = ((b * n_heads + h) * tq + t) * MOBA_TOPK
        scores = []
        for n in range(npg):
            kpos0 = sel_ref[base + n // ppb] * MOBA_BLOCK + (n % ppb) * page
            s = jnp.sum(kbuf[slot, t * npg + n] * q_t, axis=1, keepdims=True)
            dist = (past_len + t - kpos0 - rowp).astype(F32)
            scores.append(s - slope * dist)
        s_own = jnp.sum(kn * q_t, axis=1, keepdims=True)
        s_own = jnp.where(rowq <= t, s_own - slope * (t - rowq).astype(F32), NEG)
        m = jnp.max(s_own, axis=0, keepdims=True)
        for s in scores:
            m = jnp.maximum(m, jnp.max(s, axis=0, keepdims=True))
        p_own = jnp.exp(s_own - m)
        l = jnp.sum(p_own, axis=0, keepdims=True)
        acc = jnp.sum(p_own * vn, axis=0, keepdims=True)
        for n, s in enumerate(scores):
            p = jnp.exp(s - m)
            l = l + jnp.sum(p, axis=0, keepdims=True)
            acc = acc + jnp.sum(p * vbuf[slot, t * npg + n], axis=0, keepdims=True)
        o_ref[pl.ds(t, 1), :] = acc / l
        return carry

    lax.fori_loop(0, tq, body, 0)


def _moba_sample(z3, k3, v3, cache_k, cache_v, layer, pt_flat, sel_flat, slopes, n_pages, q_group):
    b, tq, _ = z3.shape
    _, _, page, n_heads, d = cache_k.shape
    ppb = MOBA_BLOCK // page
    n_bufs = tq * MOBA_TOPK * ppb
    new_rows = pl.BlockSpec((None, tq, d), lambda bi, h, pt, sel: (bi, 0, h))
    return pl.pallas_call(
        functools.partial(_moba_sample_kernel, layer=layer, n_heads=n_heads, n_pages=n_pages,
                          page=page, ppb=ppb, tq=tq),
        out_shape=jax.ShapeDtypeStruct((b, tq, n_heads * d), F32),
        grid_spec=pltpu.PrefetchScalarGridSpec(
            num_scalar_prefetch=2,
            grid=(b, n_heads),
            in_specs=[pl.BlockSpec(memory_space=pltpu.SMEM),
                      pl.BlockSpec((None, tq, d), lambda bi, h, pt, sel: (bi, 0, q_group * n_heads + h)),
                      new_rows, new_rows,
                      pl.BlockSpec(memory_space=pl.ANY), pl.BlockSpec(memory_space=pl.ANY)],
            out_specs=pl.BlockSpec((None, tq, d), lambda bi, h, pt, sel: (bi, 0, h)),
            scratch_shapes=[pltpu.VMEM((2, n_bufs, page, d), F32), pltpu.VMEM((2, n_bufs, page, d), F32),
                            pltpu.SemaphoreType.DMA((2, 2))]),
        compiler_params=_params("arbitrary", "arbitrary"),
        name="moba_sample",
    )(pt_flat, sel_flat, slopes, z3, k3, v3, cache_k, cache_v)


def _pool_kernel(x_ref, halo_ref, g0_ref, g1_ref, w_ref, sc_ref, y_ref, st_ref, ext_ref, a_ref, b_ref,
                 *, tm, group_w, halo_is_state, mxu_dtype):
    i = pl.program_id(1)
    x = x_ref[...]
    h = _rms(x, g0_ref[...])
    if halo_is_state:
        ext_ref[0:POOL_HALO - POOL_STATE, :] = jnp.zeros((POOL_HALO - POOL_STATE, x.shape[1]), F32)
        ext_ref[POOL_HALO - POOL_STATE:POOL_HALO, :] = halo_ref[...]
    else:
        ext_ref[0:POOL_HALO, :] = jnp.where(i > 0, _rms(halo_ref[...], g0_ref[...]), 0.0)
    ext_ref[POOL_HALO:POOL_HALO + tm, :] = h

    r_end = POOL_HALO + tm
    g1c, g2c, g3c = group_w, 2 * group_w, 3 * group_w
    a_ref[8:r_end, :] = ext_ref[8:r_end, :] + ext_ref[7:r_end - 1, :]
    b_ref[16:r_end, g1c:] = a_ref[16:r_end, g1c:] + a_ref[14:r_end - 2, g1c:]
    a_ref[24:r_end, g2c:] = b_ref[24:r_end, g2c:] + b_ref[20:r_end - 4, g2c:]
    sums = [a_ref[POOL_HALO:r_end, 0:g1c], b_ref[POOL_HALO:r_end, g1c:g2c],
            a_ref[POOL_HALO:r_end, g2c:g3c],
            a_ref[POOL_HALO:r_end, g3c:] + a_ref[POOL_HALO - 8:r_end - 8, g3c:]]

    t = i * tm + lax.broadcasted_iota(jnp.int32, (tm, 1), 0)
    ys = []
    for gi, win in enumerate(POOL_WINDOWS):
        cols = slice(gi * group_w, (gi + 1) * group_w)
        acc = sums[gi]
        if halo_is_state:
            mean = acc * (1.0 / win)
        else:
            mean = acc / jnp.minimum(t + 1, win).astype(F32)
        pooled = mean - h[:, cols]
        ys.append(jnp.dot(pooled.astype(mxu_dtype), w_ref[gi].astype(mxu_dtype),
                          preferred_element_type=F32))
    y = jnp.concatenate(ys, axis=1) * sc_ref[...]
    y_ref[...] = x + _rms(y, g1_ref[...])
    st_ref[...] = ext_ref[POOL_HALO + tm - POOL_STATE:POOL_HALO + tm, :]


def _pool(x3, state, g0, g1, w_pool, layer, scale, tm, mxu_dtype):
    b, t, d = x3.shape
    tm = _tile(t, tm, POOL_HALO)
    group_w = d // len(POOL_WINDOWS)
    halo_is_state = state is not None
    if halo_is_state:
        assert t == tm
        halo, halo_spec = state, pl.BlockSpec((None, POOL_STATE, d), lambda bi, i: (bi, 0, 0))
    else:
        per = tm // POOL_HALO
        halo = x3
        halo_spec = pl.BlockSpec((None, POOL_HALO, d), lambda bi, i: (bi, jnp.maximum(i * per - 1, 0), 0))
    vec = pl.BlockSpec((1, d), lambda bi, i: (0, 0))
    return pl.pallas_call(
        functools.partial(_pool_kernel, tm=tm, group_w=group_w, halo_is_state=halo_is_state,
                          mxu_dtype=mxu_dtype),
        out_shape=(jax.ShapeDtypeStruct((b, t, d), F32), jax.ShapeDtypeStruct((b, POOL_STATE, d), F32)),
        grid=(b, t // tm),
        in_specs=[pl.BlockSpec((None, tm, d), lambda bi, i: (bi, i, 0)), halo_spec, vec, vec,
                  pl.BlockSpec((None,) + w_pool.shape[1:], lambda bi, i: (layer, 0, 0, 0)), vec],
        out_specs=(pl.BlockSpec((None, tm, d), lambda bi, i: (bi, i, 0)),
                   pl.BlockSpec((None, POOL_STATE, d), lambda bi, i: (bi, 0, 0))),
        scratch_shapes=[pltpu.VMEM((POOL_HALO + tm, d), F32)] * 3,
        compiler_params=_params("parallel", "arbitrary"),
        name="pool",
    )(x3, halo, g0.reshape(1, d), g1.reshape(1, d), w_pool, scale.reshape(1, d))


def _trunk(x3, ret0, pool0, paged, weights, lg, slopes, *, row_tile, small, side_cache=None):
    w_in, w_out, w_pool_bf, w_pool, pool_scale, w_up, w_down, mlp_w, norm_g = weights
    mlp_w = list(mlp_w)
    b, t, d = x3.shape
    depth = w_up.shape[0]
    n_ret = ret0.shape[2]
    n_even = ret0.shape[0]
    group_w = n_ret * HEAD_DIM
    x = x3.reshape(b * t, d)
    mxu_small = F32 if small else BF16
    new_k, new_v, new_ret, new_pool, ksum_parts = [], [], [], [], []
    for l in range(depth):
        if l % 2 == 0:
            e = l // 2
            z, k_new, v_new = _inproj(x, norm_g[l, 0], w_in, e, group_w, 2 * row_tile, 1024)
            z3 = z.reshape(b, t, -1)
            k3 = k_new.reshape(b, t, group_w)
            v3 = v_new.reshape(b, t, group_w)
            todo = [n for n in (l, l + 1) if n < depth and mlp_w[n] is None]
            o_r, s_new, casts = _retention(z3, ret0[e], lg, n_ret, 256, mxu_small,
                                           F32 if small else BF16,
                                           (w_up, w_down, todo) if todo else None)
            if todo and casts is None:
                casts = [(w_up[n].astype(BF16), w_down[n].astype(BF16)) for n in todo]
            for n, (wu, wd) in zip(todo, casts or []):
                mlp_w[n] = (wu[None], wd[None], 0)
            new_k.append(k3.reshape(b, t, n_ret, HEAD_DIM))
            new_v.append(v3.reshape(b, t, n_ret, HEAD_DIM))
            if paged is None:
                o_b = _moba_prompt(z3, k3, v3, slopes, n_ret, 4)
            else:
                cache_k, cache_v, page_table, ksums = paged
                n_seq, n_pages = page_table.shape
                pt_flat = page_table.reshape(-1)
                sel = _moba_select(ksums[e], z3, 4)
                sel_flat = sel[..., :MOBA_TOPK].reshape(-1)
                o_b = _moba_sample(z3, k3, v3, cache_k, cache_v, e, pt_flat, sel_flat, slopes, n_pages, 4)
            x = _outproj(o_r.reshape(b * t, -1), o_b.reshape(b * t, -1), w_out, e, x, norm_g[l, 1],
                         row_tile)
            new_ret.append(s_new)
        else:
            o = l // 2
            wp = w_pool if small else w_pool_bf
            y3, p_new = _pool(x.reshape(b, t, d), None if pool0 is None else pool0[o],
                              norm_g[l, 0], norm_g[l, 1], wp, o, pool_scale[o], 256, mxu_small)
            x = y3.reshape(b * t, d)
            new_pool.append(p_new)
        side = None
        if side_cache is not None:
            cache_k, page_table = side_cache
            per_layer = depth // n_even
            n_side_pages = page_table.size // per_layer
            side = (cache_k, l // per_layer, page_table.reshape(-1), (l % per_layer) * n_side_pages,
                    n_side_pages)
        x = _mlp(x, norm_g[l, 2], norm_g[l, 3], *mlp_w[l], 2 * row_tile, 1024 if small else 512, side)
        if side is not None:
            x, part = x
            ksum_parts.append(part)
    ksums = None
    if side_cache is not None:
        n_seq = side_cache[1].shape[0]
        per_layer = depth // n_even
        ksums = []
        for e in range(n_even):
            ks = jnp.concatenate(ksum_parts[e * per_layer:(e + 1) * per_layer], axis=0)
            ks = ks.reshape((n_seq, -1) + ks.shape[1:])
            ksums.append(ks.transpose(0, 2, 1, 3))
    return (x.reshape(b, t, d), jnp.stack(new_k), jnp.stack(new_v), jnp.stack(new_ret),
            jnp.stack(new_pool), ksums, mlp_w)


def kernel(x_prompt, x_sample, cache_k, cache_v, state_ret, state_pool, page_table, w_in, w_out, w_pool,
           pool_scale, w_up, w_down, norm_g):
    n_ret = state_ret.shape[2]
    n_moba = cache_k.shape[3]
    assert n_moba == n_ret and w_in.shape[2] == 7 * n_ret * HEAD_DIM
    lg = jnp.log1p(-jnp.exp2(-5.0 - jnp.arange(n_ret, dtype=F32)))
    slopes = jnp.exp2(-8.0 * (jnp.arange(n_moba, dtype=F32) + 1.0) / n_moba)
    mlp_w = [None] * w_up.shape[0]
    shared = (w_in.astype(BF16), w_out.astype(BF16), w_pool.astype(BF16), w_pool, pool_scale,
              w_up, w_down)
    n_even = state_ret.shape[0]
    ret_zero = jnp.zeros((n_even, x_prompt.shape[0], n_ret, HEAD_DIM, HEAD_DIM), F32)

    y_p, k_p, v_p, ret_p, pool_p, ksums, mlp_w = _trunk(
        x_prompt, ret_zero, None, None, shared + (mlp_w, norm_g), lg, slopes,
        row_tile=512, small=False, side_cache=(cache_k, page_table))
    y_s, k_s, v_s, ret_s, pool_s, _, _ = _trunk(
        x_sample, state_ret, state_pool, (cache_k, cache_v, page_table, ksums),
        shared + (mlp_w, norm_g), lg, slopes, row_tile=512, small=True)
    return (y_p, y_s, k_p, v_p, k_s, v_s, ret_p, ret_s, pool_p, pool_s)
```

```python
import functools

import jax
import jax.numpy as jnp
from jax import lax
from jax.experimental import pallas as pl
from jax.experimental.pallas import tpu as pltpu

F32 = jnp.float32
BF16 = jnp.bfloat16

HEAD_DIM = 128
MOBA_BLOCK = 256
MOBA_TOPK = 3
POOL_WINDOWS = (2, 4, 8, 16)
POOL_HALO = 32
assert POOL_WINDOWS == (2, 4, 8, 16)
POOL_STATE = max(POOL_WINDOWS) - 1
EPS = 1e-6
NEG = -1e30
LOG2E = 1.4426950408889634
VMEM_LIMIT = 48 * 1024 * 1024
VMEM_LIMIT_BIG = 58 * 1024 * 1024

_NT = (((1,), (1,)), ((), ()))
_TN = (((0,), (0,)), ((), ()))


def _params(*sem, vmem=VMEM_LIMIT):
    return pltpu.CompilerParams(dimension_semantics=sem, vmem_limit_bytes=vmem)


def _tile(n, target, align):
    if n <= target:
        return n
    t = target - target % align
    while n % t:
        t -= align
    return t


def _rms(x, g):
    return x * lax.rsqrt(jnp.mean(x * x, axis=-1, keepdims=True) + EPS) * g


def _inproj_kernel(x_ref, g_ref, w_ref, z_ref, k_ref, v_ref, h_ref, *, n_main, n_kv):
    j = pl.program_id(1)

    @pl.when(j == 0)
    def _():
        h_ref[...] = _rms(x_ref[...], g_ref[...]).astype(BF16)

    def emit(o_ref):
        o_ref[...] = jnp.dot(h_ref[...], w_ref[...], preferred_element_type=F32)

    pl.when(j < n_main)(lambda: emit(z_ref))
    pl.when((j >= n_main) & (j < n_main + n_kv))(lambda: emit(k_ref))
    pl.when(j >= n_main + n_kv)(lambda: emit(v_ref))


def _inproj(x, g, w, layer, group_w, tm, tn):
    m, d = x.shape
    tm, tn = _tile(m, tm, 8), _tile(group_w, tn, 128)
    n_main, n_kv = 5 * group_w // tn, group_w // tn
    assert w.shape[2] == 7 * group_w
    return pl.pallas_call(
        functools.partial(_inproj_kernel, n_main=n_main, n_kv=n_kv),
        out_shape=(jax.ShapeDtypeStruct((m, 5 * group_w), F32),
                   jax.ShapeDtypeStruct((m, group_w), F32),
                   jax.ShapeDtypeStruct((m, group_w), F32)),
        grid=(m // tm, n_main + 2 * n_kv),
        in_specs=[pl.BlockSpec((tm, d), lambda i, j: (i, 0)),
                  pl.BlockSpec((1, d), lambda i, j: (0, 0)),
                  pl.BlockSpec((None, d, tn), lambda i, j: (layer, 0, j))],
        out_specs=(pl.BlockSpec((tm, tn), lambda i, j: (i, jnp.minimum(j, n_main - 1))),
                   pl.BlockSpec((tm, tn), lambda i, j: (i, jnp.clip(j - n_main, 0, n_kv - 1))),
                   pl.BlockSpec((tm, tn), lambda i, j: (i, jnp.clip(j - n_main - n_kv, 0, n_kv - 1)))),
        scratch_shapes=[pltpu.VMEM((tm, d), BF16)],
        compiler_params=_params("parallel", "arbitrary", vmem=VMEM_LIMIT_BIG),
        name="in_proj",
    )(x, g.reshape(1, d), w)


def _mlp_kernel(*refs, n_side, ppb):
    pages = refs[6:6 + n_side]
    x_ref, g2_ref, g3_ref, wu_ref, wd_ref = refs[1:6]
    if n_side:
        o_ref, ks_ref, h_ref = refs[6 + n_side:]
    else:
        o_ref, h_ref = refs[6:]
    f = pl.program_id(1)

    @pl.when(f == 0)
    def _():
        h_ref[...] = _rms(x_ref[...], g2_ref[...]).astype(BF16)
        o_ref[...] = jnp.zeros_like(o_ref)

    u = jnp.dot(h_ref[...], wu_ref[...], preferred_element_type=F32)
    a = jnp.square(jnp.maximum(u, 0.0)).astype(BF16)
    o_ref[...] += jnp.dot(a, wd_ref[...], preferred_element_type=F32)

    for blk in range(n_side // ppb):
        s = jnp.sum(pages[blk * ppb][...], axis=0)
        for r in range(1, ppb):
            s = s + jnp.sum(pages[blk * ppb + r][...], axis=0)
        ks_ref[blk] = s

    @pl.when(f == pl.num_programs(1) - 1)
    def _():
        o_ref[...] = x_ref[...] + _rms(o_ref[...], g3_ref[...])


def _mlp(x, g2, g3, w_up, w_down, layer, tm, tf, side=None):
    m, d = x.shape
    ff = w_up.shape[2]
    tm, tf = _tile(m, tm, 8), _tile(ff, tf, 128)
    nf = ff // tf
    steps = (m // tm) * nf
    in_specs = [pl.BlockSpec((tm, d), lambda i, f, pt: (i, 0)),
                pl.BlockSpec((1, d), lambda i, f, pt: (0, 0)),
                pl.BlockSpec((1, d), lambda i, f, pt: (0, 0)),
                pl.BlockSpec((None, d, tf), lambda i, f, pt: (layer, 0, f)),
                pl.BlockSpec((None, tf, d), lambda i, f, pt: (layer, f, 0))]
    out_shape = [jax.ShapeDtypeStruct((m, d), F32)]
    out_specs = [pl.BlockSpec((tm, d), lambda i, f, pt: (i, 0))]
    args = [x, g2.reshape(1, d), g3.reshape(1, d), w_up, w_down]
    n_side, ppb = 0, 1
    pt_flat = jnp.zeros((1,), jnp.int32)
    if side is not None:
        cache, cache_layer, pt_flat, first_page, n_side_pages = side
        _, _, page, n_heads, hd = cache.shape
        ppb = MOBA_BLOCK // page
        n_side = n_side_pages // steps
        assert n_side * steps == n_side_pages and n_side % ppb == 0 and first_page % ppb == 0

        def page_spec(r):
            return pl.BlockSpec(
                (None, None, page, n_heads, hd),
                lambda i, f, pt: (cache_layer, pt[first_page + (i * nf + f) * n_side + r], 0, 0, 0))

        in_specs += [page_spec(r) for r in range(n_side)]
        args += [cache] * n_side
        out_shape.append(jax.ShapeDtypeStruct((n_side_pages // ppb, n_heads, hd), F32))
        out_specs.append(pl.BlockSpec((n_side // ppb, n_heads, hd), lambda i, f, pt: (i * nf + f, 0, 0)))
    out = pl.pallas_call(
        functools.partial(_mlp_kernel, n_side=n_side, ppb=ppb),
        out_shape=tuple(out_shape),
        grid_spec=pltpu.PrefetchScalarGridSpec(
            num_scalar_prefetch=1,
            grid=(m // tm, nf),
            in_specs=in_specs,
            out_specs=tuple(out_specs),
            scratch_shapes=[pltpu.VMEM((tm, d), BF16)]),
        compiler_params=_params("parallel", "arbitrary", vmem=VMEM_LIMIT_BIG),
        name="mlp",
    )(pt_flat, *args)
    return out if side is not None else out[0]


def _outproj_kernel(or_ref, ob_ref, w_ref, x_ref, g_ref, o_ref, *, ret_w):
    m = jnp.dot(or_ref[...].astype(BF16), w_ref[0:ret_w, :], preferred_element_type=F32)
    m = m + jnp.dot(ob_ref[...].astype(BF16), w_ref[ret_w:, :], preferred_element_type=F32)
    o_ref[...] = x_ref[...] + _rms(m, g_ref[...])


def _outproj(o_r, o_b, w_out, layer, x, g, tm):
    m, d = x.shape
    ret_w, moba_w = o_r.shape[1], o_b.shape[1]
    tm = _tile(m, tm, 8)
    return pl.pallas_call(
        functools.partial(_outproj_kernel, ret_w=ret_w),
        out_shape=jax.ShapeDtypeStruct((m, d), F32),
        grid=(m // tm,),
        in_specs=[pl.BlockSpec((tm, ret_w), lambda i: (i, 0)),
                  pl.BlockSpec((tm, moba_w), lambda i: (i, 0)),
                  pl.BlockSpec((None, ret_w + moba_w, d), lambda i: (layer, 0, 0)),
                  pl.BlockSpec((tm, d), lambda i: (i, 0)),
                  pl.BlockSpec((1, d), lambda i: (0, 0))],
        out_specs=pl.BlockSpec((tm, d), lambda i: (i, 0)),
        compiler_params=_params("parallel"),
        name="outproj",
    )(o_r, o_b, w_out, x, g.reshape(1, d))


RET_HEADS_PER_STEP = 8


def _retention_kernel(*refs, chunk, hg, mxu_dtype, n_cast):
    lg_ref, q_ref, k_ref, v_ref, g_ref, s0_ref = refs[:6]
    cast_in = refs[6:6 + n_cast]
    o_ref, sfin_ref = refs[6 + n_cast:8 + n_cast]
    cast_out = refs[8 + n_cast:8 + 2 * n_cast]
    s_ref, decay_ref = refs[8 + 2 * n_cast:]
    for src, dst in zip(cast_in, cast_out):
        dst[...] = src[...].astype(dst.dtype)
    d = HEAD_DIM
    g = pl.program_id(1)
    c = pl.program_id(2)
    heads = range(hg)
    lgs = [lg_ref[g * hg + u] for u in heads]

    @pl.when(c == 0)
    def _():
        s_ref[...] = s0_ref[...]
        ii = lax.broadcasted_iota(jnp.int32, (chunk, chunk), 0)
        jj = lax.broadcasted_iota(jnp.int32, (chunk, chunk), 1)
        diff = (ii - jj).astype(F32)
        for u in heads:
            decay_ref[u] = jnp.where(diff >= 0, jnp.exp(lgs[u] * jnp.maximum(diff, 0.0)), 0.0)

    idx = lax.broadcasted_iota(jnp.int32, (chunk, 1), 0).astype(F32)
    cols = [slice(u * d, (u + 1) * d) for u in heads]
    ks = [k_ref[:, cols[u]] * (d ** -0.5) for u in heads]
    qm = [q_ref[:, cols[u]].astype(mxu_dtype) for u in heads]
    vm = [v_ref[:, cols[u]].astype(mxu_dtype) for u in heads]
    states = [s_ref[u] for u in heads]

    raw = [lax.dot_general(qm[u], ks[u].astype(mxu_dtype), _NT, preferred_element_type=F32)
           for u in heads]
    qs = [jnp.dot(qm[u], states[u].astype(mxu_dtype), preferred_element_type=F32) for u in heads]
    inner = [(raw[u] * decay_ref[u]).astype(mxu_dtype) for u in heads]
    kw = [(ks[u] * jnp.exp(lgs[u] * (chunk - 1.0 - idx))).astype(mxu_dtype) for u in heads]
    intra = [jnp.dot(inner[u], vm[u], preferred_element_type=F32) for u in heads]
    kv = [lax.dot_general(kw[u], vm[u], _TN, preferred_element_type=F32) for u in heads]

    last = c == pl.num_programs(2) - 1
    for u in heads:
        o = intra[u] + qs[u] * jnp.exp(lgs[u] * (idx + 1.0))
        s_new = jnp.exp(jnp.full((1, d), lgs[u] * chunk, F32)) * states[u] + kv[u]
        s_ref[u] = s_new
        mu = jnp.mean(o, axis=-1, keepdims=True)
        oc = o - mu
        var = jnp.mean(oc * oc, axis=-1, keepdims=True)
        gate = g_ref[:, cols[u]]
        silu = gate / (1.0 + jnp.exp(-gate))
        o_ref[:, cols[u]] = (oc * lax.rsqrt(var + EPS) * silu).astype(o_ref.dtype)

        @pl.when(last)
        def _():
            sfin_ref[u] = s_new


def _retention(z3, s0, lg, n_heads, chunk, mxu_dtype, out_dtype, cast=None):
    b, t, _ = z3.shape
    d = HEAD_DIM
    chunk = _tile(t, chunk, 8)
    hg = _tile(n_heads, RET_HEADS_PER_STEP, 1)
    n_groups = n_heads // hg
    nc = t // chunk
    steps = b * n_groups * nc

    def col(group):
        return pl.BlockSpec((None, chunk, hg * d), lambda bi, g, c: (bi, c, group * n_groups + g))

    state_spec = pl.BlockSpec((None, hg, d, d), lambda bi, g, c: (bi, g, 0, 0))
    in_specs = [pl.BlockSpec(memory_space=pltpu.SMEM), col(0), col(1), col(2), col(3), state_spec]
    out_shape = [jax.ShapeDtypeStruct((b, t, n_heads * d), out_dtype),
                 jax.ShapeDtypeStruct((b, n_heads, d, d), F32)]
    out_specs = [pl.BlockSpec((None, chunk, hg * d), lambda bi, g, c: (bi, c, g)), state_spec]
    args = [lg, z3, z3, z3, z3, s0]
    n_cast = 0
    if cast is not None and cast[0].shape[2] % (steps * 128) == 0:
        w_up, w_down, layers = cast
        _, dm, ff = w_up.shape
        tf = ff // steps
        n_cast = 2 * len(layers)

        def step(bi, g, c):
            return (bi * n_groups + g) * nc + c

        def up_spec(layer):
            return pl.BlockSpec((None, dm, tf), lambda bi, g, c: (layer, 0, step(bi, g, c)))

        def down_spec(layer):
            return pl.BlockSpec((None, tf, dm), lambda bi, g, c: (layer, step(bi, g, c), 0))

        for layer in layers:
            in_specs += [up_spec(layer), down_spec(layer)]
            out_shape += [jax.ShapeDtypeStruct((dm, ff), BF16), jax.ShapeDtypeStruct((ff, dm), BF16)]
            out_specs += [pl.BlockSpec((dm, tf), lambda bi, g, c: (0, step(bi, g, c))),
                          pl.BlockSpec((tf, dm), lambda bi, g, c: (step(bi, g, c), 0))]
            args += [w_up, w_down]
    out = pl.pallas_call(
        functools.partial(_retention_kernel, chunk=chunk, hg=hg, mxu_dtype=mxu_dtype, n_cast=n_cast),
        out_shape=tuple(out_shape),
        grid=(b, n_groups, nc),
        in_specs=in_specs,
        out_specs=tuple(out_specs),
        scratch_shapes=[pltpu.VMEM((hg, d, d), F32), pltpu.VMEM((hg, chunk, chunk), F32)],
        compiler_params=_params("parallel", "parallel", "arbitrary"),
        name="retention",
    )(*args)
    casts = [(out[2 + 2 * n], out[3 + 2 * n]) for n in range(n_cast // 2)]
    return out[0], out[1], casts or None


def _topk_rank(bs, n_blocks, n_past, axis):
    idx = lax.broadcasted_iota(jnp.int32, bs.shape, axis)
    cnt = jnp.zeros(bs.shape, jnp.int32)
    for n in range(n_blocks):
        sc = bs[n:n + 1, :] if axis == 0 else bs[:, n:n + 1]
        beats = jnp.where(sc > bs, 1, jnp.where((sc == bs) & (n < idx), 1, 0))
        cnt = cnt + jnp.where(n < n_past, beats, 0)
    return cnt, idx


MOBA_HEADS_PER_STEP = 8
SUM_ROWS = 16


def _moba_prompt_kernel(slopes_ref, q_ref, k_ref, v_ref, o_ref, vt_ref, kmean_ref, bias_ref,
                        alibi_ref, *, nb, nbp, hg):
    blk, d = MOBA_BLOCK, HEAD_DIM
    g = pl.program_id(1)
    i = pl.program_id(2)
    heads = range(hg)

    ones_row = jnp.where(lax.broadcasted_iota(jnp.int32, (SUM_ROWS, blk), 0) == 0, 1.0, 0.0).astype(BF16)

    @pl.when(i == 0)
    def _():
        kmean_ref[...] = jnp.zeros_like(kmean_ref)
        for u in heads:
            for n in range(nb):
                kblk = k_ref[n * blk:(n + 1) * blk, u * d:(u + 1) * d]
                kmean_ref[u, n:n + 1, :] = jnp.mean(kblk, axis=0, keepdims=True)
                vt_ref[u, n, 0:d, :] = v_ref[n * blk:(n + 1) * blk, u * d:(u + 1) * d].T.astype(BF16)
                vt_ref[u, n, d:, :] = ones_row

    def kblock(u, n):
        return k_ref[pl.ds(pl.multiple_of(n * blk, blk), blk), u * d:(u + 1) * d].astype(BF16)

    kk = lax.broadcasted_iota(jnp.int32, (blk, blk), 0)
    qq = lax.broadcasted_iota(jnp.int32, (blk, blk), 1)
    slopes = [slopes_ref[g * hg + u] * LOG2E for u in heads]
    qtbs, carry0 = [], []
    for u in heads:
        qt = q_ref[:, u * d:(u + 1) * d].T
        bs = jnp.dot(kmean_ref[u], qt, precision=lax.Precision.HIGHEST,
                     preferred_element_type=F32)
        cnt, bidx = _topk_rank(bs, nbp, i, 0)
        bias_ref[u] = jnp.where((bidx < i) & (cnt < MOBA_TOPK), 0.0, NEG)
        qtb = (qt * (d ** -0.5 * LOG2E)).astype(BF16)
        alibi = slopes[u] * kk.astype(F32)
        alibi_ref[u] = alibi
        s = jnp.dot(kblock(u, i), qtb, preferred_element_type=F32)
        s = jnp.where(kk <= qq, s + alibi, NEG)
        m0 = jnp.max(s, axis=0, keepdims=True)
        p = jnp.exp2(s - m0)
        acc0 = jnp.dot(vt_ref[u, i], p.astype(BF16), preferred_element_type=F32)
        qtbs.append(qtb)
        carry0 += [m0, acc0]

    def make_body(width):
        def body(jj, carry):
            js = [jj * width + w for w in range(width)]
            raw = [[jnp.dot(kblock(u, j), qtbs[u], preferred_element_type=F32) for u in heads] for j in js]
            stats, probs = [], []
            for u in heads:
                m = carry[2 * u]
                brows = [bias_ref[u, pl.ds(j, 1), :] - slopes[u] * ((i - j) * blk).astype(F32) for j in js]
                sjs = [raw[w][u] + alibi_ref[u] for w in range(width)]
                m_new = m
                for sj, brow in zip(sjs, brows):
                    m_new = jnp.maximum(m_new, jnp.max(sj, axis=0, keepdims=True) + brow)
                stats.append((m_new, jnp.exp2(m - m_new)))
                probs.append([jnp.exp2(sj - (m_new - brow)).astype(BF16) for sj, brow in zip(sjs, brows)])
            out = []
            for u in heads:
                m_new, alpha = stats[u]
                acc = alpha * carry[2 * u + 1]
                for w, j in enumerate(js):
                    acc = acc + jnp.dot(vt_ref[u, j], probs[u][w], preferred_element_type=F32)
                out += [m_new, acc]
            return tuple(out)
        return body

    fin = lax.fori_loop(0, i // 2, make_body(2), tuple(carry0))
    fin = lax.fori_loop(2 * (i // 2), i, make_body(1), fin)
    for u in heads:
        acc = fin[2 * u + 1]
        o_ref[:, u * d:(u + 1) * d] = (acc[0:d] / acc[d:d + 1]).T.astype(o_ref.dtype)


def _moba_prompt(z3, k3, v3, slopes, n_heads, q_group):
    b, t, _ = z3.shape
    blk, d = MOBA_BLOCK, HEAD_DIM
    nb = t // blk
    nbp = -(-nb // 8) * 8
    hg = _tile(n_heads, MOBA_HEADS_PER_STEP, 1)
    n_groups = n_heads // hg
    whole = pl.BlockSpec((None, t, hg * d), lambda bi, g, i: (bi, 0, g), pipeline_mode=pl.Buffered(1))
    return pl.pallas_call(
        functools.partial(_moba_prompt_kernel, nb=nb, nbp=nbp, hg=hg),
        out_shape=jax.ShapeDtypeStruct((b, t, n_heads * d), BF16),
        grid=(b, n_groups, nb),
        in_specs=[pl.BlockSpec(memory_space=pltpu.SMEM),
                  pl.BlockSpec((None, blk, hg * d), lambda bi, g, i: (bi, i, q_group * n_groups + g)),
                  whole, whole],
        out_specs=pl.BlockSpec((None, blk, hg * d), lambda bi, g, i: (bi, i, g)),
        scratch_shapes=[pltpu.VMEM((hg, nb, d + SUM_ROWS, blk), BF16),
                        pltpu.VMEM((hg, nbp, d), F32), pltpu.VMEM((hg, nbp, blk), F32),
                        pltpu.VMEM((hg, blk, blk), F32)],
        compiler_params=_params("parallel", "parallel", "arbitrary", vmem=VMEM_LIMIT_BIG),
        name="moba_prompt",
    )(slopes, z3, k3, v3)


def _moba_select_kernel(ks_ref, q_ref, o_ref, *, n_heads, nb):
    q = q_ref[...]
    lane = lax.broadcasted_iota(jnp.int32, (q.shape[0], HEAD_DIM), 1)
    for h in range(n_heads):
        kmean = ks_ref[h] * (1.0 / MOBA_BLOCK)
        bs = lax.dot_general(q[:, h * HEAD_DIM:(h + 1) * HEAD_DIM], kmean, _NT,
                             precision=lax.Precision.HIGHEST, preferred_element_type=F32)
        cnt, col = _topk_rank(bs, nb, nb, 1)
        out = jnp.zeros(lane.shape, jnp.int32)
        for r in range(MOBA_TOPK):
            idx = jnp.sum(jnp.where(cnt == r, col, 0).astype(F32), axis=1, keepdims=True)
            out = jnp.where(lane == r, idx.astype(jnp.int32), out)
        o_ref[h] = out


def _moba_select(ksum, z3, q_group):
    b, n_heads, nb, d = ksum.shape
    tq = z3.shape[1]
    assert nb >= MOBA_TOPK and tq <= MOBA_BLOCK
    return pl.pallas_call(
        functools.partial(_moba_select_kernel, n_heads=n_heads, nb=nb),
        out_shape=jax.ShapeDtypeStruct((b, n_heads, tq, HEAD_DIM), jnp.int32),
        grid=(b,),
        in_specs=[pl.BlockSpec((None, n_heads, nb, d), lambda bi: (bi, 0, 0, 0)),
                  pl.BlockSpec((None, tq, n_heads * d), lambda bi: (bi, 0, q_group))],
        out_specs=pl.BlockSpec((None, n_heads, tq, HEAD_DIM), lambda bi: (bi, 0, 0, 0)),
        compiler_params=_params("parallel"),
        name="moba_select",
    )(ksum, z3)


def _moba_sample_kernel(pt_ref, sel_ref, slopes_ref, q_ref, kn_ref, vn_ref, ck_ref, cv_ref, o_ref,
                        kbuf, vbuf, sem, *, layer, n_heads, n_pages, page, ppb, tq):
    b = pl.program_id(0)
    h = pl.program_id(1)
    step = b * n_heads + h
    n_steps = pl.num_programs(0) * n_heads
    npg = MOBA_TOPK * ppb
    slot = step % 2

    def copies(bi, hi, to_slot):
        out = []
        for t in range(tq):
            for n in range(npg):
                blk = sel_ref[((bi * n_heads + hi) * tq + t) * MOBA_TOPK + n // ppb]
                pg = pt_ref[bi * n_pages + blk * ppb + n % ppb]
                out.append(pltpu.make_async_copy(ck_ref.at[layer, pg, :, hi, :],
                                                 kbuf.at[to_slot, t * npg + n], sem.at[to_slot, 0]))
                out.append(pltpu.make_async_copy(cv_ref.at[layer, pg, :, hi, :],
                                                 vbuf.at[to_slot, t * npg + n], sem.at[to_slot, 1]))
        return out

    @pl.when(step == 0)
    def _():
        for cp in copies(b, h, slot):
            cp.start()

    @pl.when(step + 1 < n_steps)
    def _():
        nxt = step + 1
        for cp in copies(nxt // n_heads, nxt % n_heads, 1 - slot):
            cp.start()

    for cp in copies(b, h, slot):
        cp.wait()

    slope = slopes_ref[h]
    past_len = n_pages * page
    rowp = lax.broadcasted_iota(jnp.int32, (page, 1), 0)
    rowq = lax.broadcasted_iota(jnp.int32, (tq, 1), 0)
    kn = kn_ref[...]
    vn = vn_ref[...]

    def body(t, carry):
        q_t = q_ref[pl.ds(t, 1), :] * (HEAD_DIM ** -0.5)
        base = ((b * n_heads + h) * tq + t) * MOBA_TOPK
        scores = []
        for n in range(npg):
            kpos0 = sel_ref[base + n // ppb] * MOBA_BLOCK + (n % ppb) * page
            s = jnp.sum(kbuf[slot, t * npg + n] * q_t, axis=1, keepdims=True)
            dist = (past_len + t - kpos0 - rowp).astype(F32)
            scores.append(s - slope * dist)
        s_own = jnp.sum(kn * q_t, axis=1, keepdims=True)
        s_own = jnp.where(rowq <= t, s_own - slope * (t - rowq).astype(F32), NEG)
        m = jnp.max(s_own, axis=0, keepdims=True)
        for s in scores:
            m = jnp.maximum(m, jnp.max(s, axis=0, keepdims=True))
        p_own = jnp.exp(s_own - m)
        l = jnp.sum(p_own, axis=0, keepdims=True)
        acc = jnp.sum(p_own * vn, axis=0, keepdims=True)
        for n, s in enumerate(scores):
            p = jnp.exp(s - m)
            l = l + jnp.sum(p, axis=0, keepdims=True)
            acc = acc + jnp.sum(p * vbuf[slot, t * npg + n], axis=0, keepdims=True)
        o_ref[pl.ds(t, 1), :] = acc / l
        return carry

    lax.fori_loop(0, tq, body, 0)


def _moba_sample(z3, k3, v3, cache_k, cache_v, layer, pt_flat, sel_flat, slopes, n_pages, q_group):
    b, tq, _ = z3.shape
    _, _, page, n_heads, d = cache_k.shape
    ppb = MOBA_BLOCK // page
    n_bufs = tq * MOBA_TOPK * ppb
    new_rows = pl.BlockSpec((None, tq, d), lambda bi, h, pt, sel: (bi, 0, h))
    return pl.pallas_call(
        functools.partial(_moba_sample_kernel, layer=layer, n_heads=n_heads, n_pages=n_pages,
                          page=page, ppb=ppb, tq=tq),
        out_shape=jax.ShapeDtypeStruct((b, tq, n_heads * d), F32),
        grid_spec=pltpu.PrefetchScalarGridSpec(
            num_scalar_prefetch=2,
            grid=(b, n_heads),
            in_specs=[pl.BlockSpec(memory_space=pltpu.SMEM),
                      pl.BlockSpec((None, tq, d), lambda bi, h, pt, sel: (bi, 0, q_group * n_heads + h)),
                      new_rows, new_rows,
                      pl.BlockSpec(memory_space=pl.ANY), pl.BlockSpec(memory_space=pl.ANY)],
            out_specs=pl.BlockSpec((None, tq, d), lambda bi, h, pt, sel: (bi, 0, h)),
            scratch_shapes=[pltpu.VMEM((2, n_bufs, page, d), F32), pltpu.VMEM((2, n_bufs, page, d), F32),
                            pltpu.SemaphoreType.DMA((2, 2))]),
        compiler_params=_params("arbitrary", "arbitrary"),
        name="moba_sample",
    )(pt_flat, sel_flat, slopes, z3, k3, v3, cache_k, cache_v)


def _pool_kernel(x_ref, halo_ref, g0_ref, g1_ref, w_ref, sc_ref, y_ref, st_ref, ext_ref, a_ref, b_ref,
                 *, tm, group_w, halo_is_state, mxu_dtype):
    i = pl.program_id(1)
    x = x_ref[...]
    h = _rms(x, g0_ref[...])
    if halo_is_state:
        ext_ref[0:POOL_HALO - POOL_STATE, :] = jnp.zeros((POOL_HALO - POOL_STATE, x.shape[1]), F32)
        ext_ref[POOL_HALO - POOL_STATE:POOL_HALO, :] = halo_ref[...]
    else:
        ext_ref[0:POOL_HALO, :] = jnp.where(i > 0, _rms(halo_ref[...], g0_ref[...]), 0.0)
    ext_ref[POOL_HALO:POOL_HALO + tm, :] = h

    r_end = POOL_HALO + tm
    g1c, g2c, g3c = group_w, 2 * group_w, 3 * group_w
    a_ref[8:r_end, :] = ext_ref[8:r_end, :] + ext_ref[7:r_end - 1, :]
    b_ref[16:r_end, g1c:] = a_ref[16:r_end, g1c:] + a_ref[14:r_end - 2, g1c:]
    a_ref[24:r_end, g2c:] = b_ref[24:r_end, g2c:] + b_ref[20:r_end - 4, g2c:]
    sums = [a_ref[POOL_HALO:r_end, 0:g1c], b_ref[POOL_HALO:r_end, g1c:g2c],
            a_ref[POOL_HALO:r_end, g2c:g3c],
            a_ref[POOL_HALO:r_end, g3c:] + a_ref[POOL_HALO - 8:r_end - 8, g3c:]]

    t = i * tm + lax.broadcasted_iota(jnp.int32, (tm, 1), 0)
    ys = []
    for gi, win in enumerate(POOL_WINDOWS):
        cols = slice(gi * group_w, (gi + 1) * group_w)
        acc = sums[gi]
        if halo_is_state:
            mean = acc * (1.0 / win)
        else:
            mean = acc / jnp.minimum(t + 1, win).astype(F32)
        pooled = mean - h[:, cols]
        ys.append(jnp.dot(pooled.astype(mxu_dtype), w_ref[gi].astype(mxu_dtype),
                          preferred_element_type=F32))
    y = jnp.concatenate(ys, axis=1) * sc_ref[...]
    y_ref[...] = x + _rms(y, g1_ref[...])
    st_ref[...] = ext_ref[POOL_HALO + tm - POOL_STATE:POOL_HALO + tm, :]


def _pool(x3, state, g0, g1, w_pool, layer, scale, tm, mxu_dtype):
    b, t, d = x3.shape
    tm = _tile(t, tm, POOL_HALO)
    group_w = d // len(POOL_WINDOWS)
    halo_is_state = state is not None
    if halo_is_state:
        assert t == tm
        halo, halo_spec = state, pl.BlockSpec((None, POOL_STATE, d), lambda bi, i: (bi, 0, 0))
    else:
        per = tm // POOL_HALO
        halo = x3
        halo_spec = pl.BlockSpec((None, POOL_HALO, d), lambda bi, i: (bi, jnp.maximum(i * per - 1, 0), 0))
    vec = pl.BlockSpec((1, d), lambda bi, i: (0, 0))
    return pl.pallas_call(
        functools.partial(_pool_kernel, tm=tm, group_w=group_w, halo_is_state=halo_is_state,
                          mxu_dtype=mxu_dtype),
        out_shape=(jax.ShapeDtypeStruct((b, t, d), F32), jax.ShapeDtypeStruct((b, POOL_STATE, d), F32)),
        grid=(b, t // tm),
        in_specs=[pl.BlockSpec((None, tm, d), lambda bi, i: (bi, i, 0)), halo_spec, vec, vec,
                  pl.BlockSpec((None,) + w_pool.shape[1:], lambda bi, i: (layer, 0, 0, 0)), vec],
        out_specs=(pl.BlockSpec((None, tm, d), lambda bi, i: (bi, i, 0)),
                   pl.BlockSpec((None, POOL_STATE, d), lambda bi, i: (bi, 0, 0))),
        scratch_shapes=[pltpu.VMEM((POOL_HALO + tm, d), F32)] * 3,
        compiler_params=_params("parallel", "arbitrary"),
        name="pool",
    )(x3, halo, g0.reshape(1, d), g1.reshape(1, d), w_pool, scale.reshape(1, d))


def _trunk(x3, ret0, pool0, paged, weights, lg, slopes, *, row_tile, small, side_cache=None):
    w_in, w_out, w_pool_bf, w_pool, pool_scale, w_up, w_down, mlp_w, norm_g = weights
    mlp_w = list(mlp_w)
    b, t, d = x3.shape
    depth = w_up.shape[0]
    n_ret = ret0.shape[2]
    n_even = ret0.shape[0]
    group_w = n_ret * HEAD_DIM
    x = x3.reshape(b * t, d)
    mxu_small = F32 if small else BF16
    new_k, new_v, new_ret, new_pool, ksum_parts = [], [], [], [], []
    for l in range(depth):
        if l % 2 == 0:
            e = l // 2
            z, k_new, v_new = _inproj(x, norm_g[l, 0], w_in, e, group_w, 2 * row_tile, 1024)
            z3 = z.reshape(b, t, -1)
            k3 = k_new.reshape(b, t, group_w)
            v3 = v_new.reshape(b, t, group_w)
            todo = [n for n in (l, l + 1) if n < depth and mlp_w[n] is None]
            o_r, s_new, casts = _retention(z3, ret0[e], lg, n_ret, 256, mxu_small,
                                           F32 if small else BF16,
                                           (w_up, w_down, todo) if todo else None)
            if todo and casts is None:
                casts = [(w_up[n].astype(BF16), w_down[n].astype(BF16)) for n in todo]
            for n, (wu, wd) in zip(todo, casts or []):
                mlp_w[n] = (wu[None], wd[None], 0)
            new_k.append(k3.reshape(b, t, n_ret, HEAD_DIM))
            new_v.append(v3.reshape(b, t, n_ret, HEAD_DIM))
            if paged is None:
                o_b = _moba_prompt(z3, k3, v3, slopes, n_ret, 4)
            else:
                cache_k, cache_v, page_table, ksums = paged
                n_seq, n_pages = page_table.shape
                pt_flat = page_table.reshape(-1)
                sel = _moba_select(ksums[e], z3, 4)
                sel_flat = sel[..., :MOBA_TOPK].reshape(-1)
                o_b = _moba_sample(z3, k3, v3, cache_k, cache_v, e, pt_flat, sel_flat, slopes, n_pages, 4)
            x = _outproj(o_r.reshape(b * t, -1), o_b.reshape(b * t, -1), w_out, e, x, norm_g[l, 1],
                         row_tile)
            new_ret.append(s_new)
        else:
            o = l // 2
            wp = w_pool if small else w_pool_bf
            y3, p_new = _pool(x.reshape(b, t, d), None if pool0 is None else pool0[o],
                              norm_g[l, 0], norm_g[l, 1], wp, o, pool_scale[o], 256, mxu_small)
            x = y3.reshape(b * t, d)
            new_pool.append(p_new)
        side = None
        if side_cache is not None:
            cache_k, page_table = side_cache
            per_layer = depth // n_even
            n_side_pages = page_table.size // per_layer
            side = (cache_k, l // per_layer, page_table.reshape(-1), (l % per_layer) * n_side_pages,
                    n_side_pages)
        x = _mlp(x, norm_g[l, 2], norm_g[l, 3], *mlp_w[l], 2 * row_tile, 1024 if small else 512, side)
        if side is not None:
            x, part = x
            ksum_parts.append(part)
    ksums = None
    if side_cache is not None:
        n_seq = side_cache[1].shape[0]
        per_layer = depth // n_even
        ksums = []
        for e in range(n_even):
            ks = jnp.concatenate(ksum_parts[e * per_layer:(e + 1) * per_layer], axis=0)
            ks = ks.reshape((n_seq, -1) + ks.shape[1:])
            ksums.append(ks.transpose(0, 2, 1, 3))
    return (x.reshape(b, t, d), jnp.stack(new_k), jnp.stack(new_v), jnp.stack(new_ret),
            jnp.stack(new_pool), ksums, mlp_w)


def kernel(x_prompt, x_sample, cache_k, cache_v, state_ret, state_pool, page_table, w_in, w_out, w_pool,
           pool_scale, w_up, w_down, norm_g):
    n_ret = state_ret.shape[2]
    n_moba = cache_k.shape[3]
    assert n_moba == n_ret and w_in.shape[2] == 7 * n_ret * HEAD_DIM
    lg = jnp.log1p(-jnp.exp2(-5.0 - jnp.arange(n_ret, dtype=F32)))
    slopes = jnp.exp2(-8.0 * (jnp.arange(n_moba, dtype=F32) + 1.0) / n_moba)
    mlp_w = [None] * w_up.shape[0]
    shared = (w_in.astype(BF16), w_out.astype(BF16), w_pool.astype(BF16), w_pool, pool_scale,
              w_up, w_down)
    n_even = state_ret.shape[0]
    ret_zero = jnp.zeros((n_even, x_prompt.shape[0], n_ret, HEAD_DIM, HEAD_DIM), F32)

    y_p, k_p, v_p, ret_p, pool_p, ksums, mlp_w = _trunk(
        x_prompt, ret_zero, None, None, shared + (mlp_w, norm_g), lg, slopes,
        row_tile=512, small=False, side_cache=(cache_k, page_table))
    y_s, k_s, v_s, ret_s, pool_s, _, _ = _trunk(
        x_sample, state_ret, state_pool, (cache_k, cache_v, page_table, ksums),
        shared + (mlp_w, norm_g), lg, slopes, row_tile=512, small=True)
    return (y_p, y_s, k_p, v_p, k_s, v_s, ret_p, ret_s, pool_p, pool_s)
```

```python
import functools

import jax
import jax.numpy as jnp
from jax import lax
from jax.experimental import pallas as pl
from jax.experimental.pallas import tpu as pltpu

F32 = jnp.float32
BF16 = jnp.bfloat16

HEAD_DIM = 128
MOBA_BLOCK = 256
MOBA_TOPK = 3
POOL_WINDOWS = (2, 4, 8, 16)
POOL_HALO = 32
assert POOL_WINDOWS == (2, 4, 8, 16)
POOL_STATE = max(POOL_WINDOWS) - 1
EPS = 1e-6
NEG = -1e30
LOG2E = 1.4426950408889634
VMEM_LIMIT = 48 * 1024 * 1024
VMEM_LIMIT_BIG = 58 * 1024 * 1024

_NT = (((1,), (1,)), ((), ()))
_TN = (((0,), (0,)), ((), ()))


def _params(*sem, vmem=VMEM_LIMIT):
    return pltpu.CompilerParams(dimension_semantics=sem, vmem_limit_bytes=vmem)


def _tile(n, target, align):
    if n <= target:
        return n
    t = target - target % align
    while n % t:
        t -= align
    return t


def _rms(x, g):
    return x * lax.rsqrt(jnp.mean(x * x, axis=-1, keepdims=True) + EPS) * g


def _inproj_kernel(x_ref, g_ref, w_ref, z_ref, k_ref, v_ref, h_ref, *, n_main, n_kv):
    j = pl.program_id(1)

    @pl.when(j == 0)
    def _():
        h_ref[...] = _rms(x_ref[...], g_ref[...]).astype(BF16)

    def emit(o_ref):
        o_ref[...] = jnp.dot(h_ref[...], w_ref[...], preferred_element_type=F32)

    pl.when(j < n_main)(lambda: emit(z_ref))
    pl.when((j >= n_main) & (j < n_main + n_kv))(lambda: emit(k_ref))
    pl.when(j >= n_main + n_kv)(lambda: emit(v_ref))


def _inproj(x, g, w, layer, group_w, tm, tn):
    m, d = x.shape
    tm, tn = _tile(m, tm, 8), _tile(group_w, tn, 128)
    n_main, n_kv = 5 * group_w // tn, group_w // tn
    assert w.shape[2] == 7 * group_w
    return pl.pallas_call(
        functools.partial(_inproj_kernel, n_main=n_main, n_kv=n_kv),
        out_shape=(jax.ShapeDtypeStruct((m, 5 * group_w), F32),
                   jax.ShapeDtypeStruct((m, group_w), F32),
                   jax.ShapeDtypeStruct((m, group_w), F32)),
        grid=(m // tm, n_main + 2 * n_kv),
        in_specs=[pl.BlockSpec((tm, d), lambda i, j: (i, 0)),
                  pl.BlockSpec((1, d), lambda i, j: (0, 0)),
                  pl.BlockSpec((None, d, tn), lambda i, j: (layer, 0, j))],
        out_specs=(pl.BlockSpec((tm, tn), lambda i, j: (i, jnp.minimum(j, n_main - 1))),
                   pl.BlockSpec((tm, tn), lambda i, j: (i, jnp.clip(j - n_main, 0, n_kv - 1))),
                   pl.BlockSpec((tm, tn), lambda i, j: (i, jnp.clip(j - n_main - n_kv, 0, n_kv - 1)))),
        scratch_shapes=[pltpu.VMEM((tm, d), BF16)],
        compiler_params=_params("parallel", "arbitrary", vmem=VMEM_LIMIT_BIG),
        name="in_proj",
    )(x, g.reshape(1, d), w)


def _mlp_kernel(*refs, n_side, ppb):
    pages = refs[6:6 + n_side]
    x_ref, g2_ref, g3_ref, wu_ref, wd_ref = refs[1:6]
    if n_side:
        o_ref, ks_ref, h_ref = refs[6 + n_side:]
    else:
        o_ref, h_ref = refs[6:]
    f = pl.program_id(1)

    @pl.when(f == 0)
    def _():
        h_ref[...] = _rms(x_ref[...], g2_ref[...]).astype(BF16)
        o_ref[...] = jnp.zeros_like(o_ref)

    u = jnp.dot(h_ref[...], wu_ref[...], preferred_element_type=F32)
    a = jnp.square(jnp.maximum(u, 0.0)).astype(BF16)
    o_ref[...] += jnp.dot(a, wd_ref[...], preferred_element_type=F32)

    for blk in range(n_side // ppb):
        s = jnp.sum(pages[blk * ppb][...], axis=0)
        for r in range(1, ppb):
            s = s + jnp.sum(pages[blk * ppb + r][...], axis=0)
        ks_ref[blk] = s

    @pl.when(f == pl.num_programs(1) - 1)
    def _():
        o_ref[...] = x_ref[...] + _rms(o_ref[...], g3_ref[...])


def _mlp(x, g2, g3, w_up, w_down, layer, tm, tf, side=None):
    m, d = x.shape
    ff = w_up.shape[2]
    tm, tf = _tile(m, tm, 8), _tile(ff, tf, 128)
    nf = ff // tf
    steps = (m // tm) * nf
    in_specs = [pl.BlockSpec((tm, d), lambda i, f, pt: (i, 0)),
                pl.BlockSpec((1, d), lambda i, f, pt: (0, 0)),
                pl.BlockSpec((1, d), lambda i, f, pt: (0, 0)),
                pl.BlockSpec((None, d, tf), lambda i, f, pt: (layer, 0, f)),
                pl.BlockSpec((None, tf, d), lambda i, f, pt: (layer, f, 0))]
    out_shape = [jax.ShapeDtypeStruct((m, d), F32)]
    out_specs = [pl.BlockSpec((tm, d), lambda i, f, pt: (i, 0))]
    args = [x, g2.reshape(1, d), g3.reshape(1, d), w_up, w_down]
    n_side, ppb = 0, 1
    pt_flat = jnp.zeros((1,), jnp.int32)
    if side is not None:
        cache, cache_layer, pt_flat, first_page, n_side_pages = side
        _, _, page, n_heads, hd = cache.shape
        ppb = MOBA_BLOCK // page
        n_side = n_side_pages // steps
        assert n_side * steps == n_side_pages and n_side % ppb == 0 and first_page % ppb == 0

        def page_spec(r):
            return pl.BlockSpec(
                (None, None, page, n_heads, hd),
                lambda i, f, pt: (cache_layer, pt[first_page + (i * nf + f) * n_side + r], 0, 0, 0))

        in_specs += [page_spec(r) for r in range(n_side)]
        args += [cache] * n_side
        out_shape.append(jax.ShapeDtypeStruct((n_side_pages // ppb, n_heads, hd), F32))
        out_specs.append(pl.BlockSpec((n_side // ppb, n_heads, hd), lambda i, f, pt: (i * nf + f, 0, 0)))
    out = pl.pallas_call(
        functools.partial(_mlp_kernel, n_side=n_side, ppb=ppb),
        out_shape=tuple(out_shape),
        grid_spec=pltpu.PrefetchScalarGridSpec(
            num_scalar_prefetch=1,
            grid=(m // tm, nf),
            in_specs=in_specs,
            out_specs=tuple(out_specs),
            scratch_shapes=[pltpu.VMEM((tm, d), BF16)]),
        compiler_params=_params("parallel", "arbitrary", vmem=VMEM_LIMIT_BIG),
        name="mlp",
    )(pt_flat, *args)
    return out if side is not None else out[0]


def _outproj_kernel(or_ref, ob_ref, w_ref, x_ref, g_ref, o_ref, *, ret_w):
    m = jnp.dot(or_ref[...].astype(BF16), w_ref[0:ret_w, :], preferred_element_type=F32)
    m = m + jnp.dot(ob_ref[...].astype(BF16), w_ref[ret_w:, :], preferred_element_type=F32)
    o_ref[...] = x_ref[...] + _rms(m, g_ref[...])


def _outproj(o_r, o_b, w_out, layer, x, g, tm):
    m, d = x.shape
    ret_w, moba_w = o_r.shape[1], o_b.shape[1]
    tm = _tile(m, tm, 8)
    return pl.pallas_call(
        functools.partial(_outproj_kernel, ret_w=ret_w),
        out_shape=jax.ShapeDtypeStruct((m, d), F32),
        grid=(m // tm,),
        in_specs=[pl.BlockSpec((tm, ret_w), lambda i: (i, 0)),
                  pl.BlockSpec((tm, moba_w), lambda i: (i, 0)),
                  pl.BlockSpec((None, ret_w + moba_w, d), lambda i: (layer, 0, 0)),
                  pl.BlockSpec((tm, d), lambda i: (i, 0)),
                  pl.BlockSpec((1, d), lambda i: (0, 0))],
        out_specs=pl.BlockSpec((tm, d), lambda i: (i, 0)),
        compiler_params=_params("parallel"),
        name="outproj",
    )(o_r, o_b, w_out, x, g.reshape(1, d))


RET_HEADS_PER_STEP = 8


def _retention_kernel(*refs, chunk, hg, mxu_dtype, n_cast):
    lg_ref, q_ref, k_ref, v_ref, g_ref, s0_ref = refs[:6]
    cast_in = refs[6:6 + n_cast]
    o_ref, sfin_ref = refs[6 + n_cast:8 + n_cast]
    cast_out = refs[8 + n_cast:8 + 2 * n_cast]
    s_ref, decay_ref = refs[8 + 2 * n_cast:]
    for src, dst in zip(cast_in, cast_out):
        dst[...] = src[...].astype(dst.dtype)
    d = HEAD_DIM
    g = pl.program_id(1)
    c = pl.program_id(2)
    heads = range(hg)
    lgs = [lg_ref[g * hg + u] for u in heads]

    @pl.when(c == 0)
    def _():
        s_ref[...] = s0_ref[...]
        ii = lax.broadcasted_iota(jnp.int32, (chunk, chunk), 0)
        jj = lax.broadcasted_iota(jnp.int32, (chunk, chunk), 1)
        diff = (ii - jj).astype(F32)
        for u in heads:
            decay_ref[u] = jnp.where(diff >= 0, jnp.exp(lgs[u] * jnp.maximum(diff, 0.0)), 0.0)

    idx = lax.broadcasted_iota(jnp.int32, (chunk, 1), 0).astype(F32)
    cols = [slice(u * d, (u + 1) * d) for u in heads]
    ks = [k_ref[:, cols[u]] * (d ** -0.5) for u in heads]
    qm = [q_ref[:, cols[u]].astype(mxu_dtype) for u in heads]
    vm = [v_ref[:, cols[u]].astype(mxu_dtype) for u in heads]
    states = [s_ref[u] for u in heads]

    raw = [lax.dot_general(qm[u], ks[u].astype(mxu_dtype), _NT, preferred_element_type=F32)
           for u in heads]
    qs = [jnp.dot(qm[u], states[u].astype(mxu_dtype), preferred_element_type=F32) for u in heads]
    inner = [(raw[u] * decay_ref[u]).astype(mxu_dtype) for u in heads]
    kw = [(ks[u] * jnp.exp(lgs[u] * (chunk - 1.0 - idx))).astype(mxu_dtype) for u in heads]
    intra = [jnp.dot(inner[u], vm[u], preferred_element_type=F32) for u in heads]
    kv = [lax.dot_general(kw[u], vm[u], _TN, preferred_element_type=F32) for u in heads]

    last = c == pl.num_programs(2) - 1
    for u in heads:
        o = intra[u] + qs[u] * jnp.exp(lgs[u] * (idx + 1.0))
        s_new = jnp.exp(jnp.full((1, d), lgs[u] * chunk, F32)) * states[u] + kv[u]
        s_ref[u] = s_new
        mu = jnp.mean(o, axis=-1, keepdims=True)
        oc = o - mu
        var = jnp.mean(oc * oc, axis=-1, keepdims=True)
        gate = g_ref[:, cols[u]]
        silu = gate / (1.0 + jnp.exp(-gate))
        o_ref[:, cols[u]] = (oc * lax.rsqrt(var + EPS) * silu).astype(o_ref.dtype)

        @pl.when(last)
        def _():
            sfin_ref[u] = s_new


def _retention(z3, s0, lg, n_heads, chunk, mxu_dtype, out_dtype, cast=None):
    b, t, _ = z3.shape
    d = HEAD_DIM
    chunk = _tile(t, chunk, 8)
    hg = _tile(n_heads, RET_HEADS_PER_STEP, 1)
    n_groups = n_heads // hg
    nc = t // chunk
    steps = b * n_groups * nc

    def col(group):
        return pl.BlockSpec((None, chunk, hg * d), lambda bi, g, c: (bi, c, group * n_groups + g))

    state_spec = pl.BlockSpec((None, hg, d, d), lambda bi, g, c: (bi, g, 0, 0))
    in_specs = [pl.BlockSpec(memory_space=pltpu.SMEM), col(0), col(1), col(2), col(3), state_spec]
    out_shape = [jax.ShapeDtypeStruct((b, t, n_heads * d), out_dtype),
                 jax.ShapeDtypeStruct((b, n_heads, d, d), F32)]
    out_specs = [pl.BlockSpec((None, chunk, hg * d), lambda bi, g, c: (bi, c, g)), state_spec]
    args = [lg, z3, z3, z3, z3, s0]
    n_cast = 0
    if cast is not None and cast[0].shape[2] % (steps * 128) == 0:
        w_up, w_down, layers = cast
        _, dm, ff = w_up.shape
        tf = ff // steps
        n_cast = 2 * len(layers)

        def step(bi, g, c):
            return (bi * n_groups + g) * nc + c

        def up_spec(layer):
            return pl.BlockSpec((None, dm, tf), lambda bi, g, c: (layer, 0, step(bi, g, c)))

        def down_spec(layer):
            return pl.BlockSpec((None, tf, dm), lambda bi, g, c: (layer, step(bi, g, c), 0))

        for layer in layers:
            in_specs += [up_spec(layer), down_spec(layer)]
            out_shape += [jax.ShapeDtypeStruct((dm, ff), BF16), jax.ShapeDtypeStruct((ff, dm), BF16)]
            out_specs += [pl.BlockSpec((dm, tf), lambda bi, g, c: (0, step(bi, g, c))),
                          pl.BlockSpec((tf, dm), lambda bi, g, c: (step(bi, g, c), 0))]
            args += [w_up, w_down]
    out = pl.pallas_call(
        functools.partial(_retention_kernel, chunk=chunk, hg=hg, mxu_dtype=mxu_dtype, n_cast=n_cast),
        out_shape=tuple(out_shape),
        grid=(b, n_groups, nc),
        in_specs=in_specs,
        out_specs=tuple(out_specs),
        scratch_shapes=[pltpu.VMEM((hg, d, d), F32), pltpu.VMEM((hg, chunk, chunk), F32)],
        compiler_params=_params("parallel", "parallel", "arbitrary"),
        name="retention",
    )(*args)
    casts = [(out[2 + 2 * n], out[3 + 2 * n]) for n in range(n_cast // 2)]
    return out[0], out[1], casts or None


def _topk_rank(bs, n_blocks, n_past, axis):
    idx = lax.broadcasted_iota(jnp.int32, bs.shape, axis)
    cnt = jnp.zeros(bs.shape, jnp.int32)
    for n in range(n_blocks):
        sc = bs[n:n + 1, :] if axis == 0 else bs[:, n:n + 1]
        beats = jnp.where(sc > bs, 1, jnp.where((sc == bs) & (n < idx), 1, 0))
        cnt = cnt + jnp.where(n < n_past, beats, 0)
    return cnt, idx


MOBA_HEADS_PER_STEP = 8
SUM_ROWS = 16


def _moba_prompt_kernel(slopes_ref, q_ref, k_ref, v_ref, o_ref, vt_ref, kmean_ref, bias_ref,
                        alibi_ref, *, nb, nbp, hg):
    blk, d = MOBA_BLOCK, HEAD_DIM
    g = pl.program_id(1)
    i = pl.program_id(2)
    heads = range(hg)

    ones_row = jnp.where(lax.broadcasted_iota(jnp.int32, (SUM_ROWS, blk), 0) == 0, 1.0, 0.0).astype(BF16)

    @pl.when(i == 0)
    def _():
        kmean_ref[...] = jnp.zeros_like(kmean_ref)
        key_offset = lax.broadcasted_iota(jnp.int32, (blk, blk), 0).astype(F32)
        for u in heads:
            alibi_ref[u] = (slopes_ref[g * hg + u] * LOG2E) * key_offset
            for n in range(nb):
                kblk = k_ref[n * blk:(n + 1) * blk, u * d:(u + 1) * d]
                kmean_ref[u, n:n + 1, :] = jnp.mean(kblk, axis=0, keepdims=True)
                vt_ref[u, n, 0:d, :] = v_ref[n * blk:(n + 1) * blk, u * d:(u + 1) * d].T.astype(BF16)
                vt_ref[u, n, d:, :] = ones_row

    def kblock(u, n):
        return k_ref[pl.ds(pl.multiple_of(n * blk, blk), blk), u * d:(u + 1) * d].astype(BF16)

    kk = lax.broadcasted_iota(jnp.int32, (blk, blk), 0)
    qq = lax.broadcasted_iota(jnp.int32, (blk, blk), 1)
    slopes = [slopes_ref[g * hg + u] * LOG2E for u in heads]
    qtbs, carry0 = [], []
    for u in heads:
        qt = q_ref[:, u * d:(u + 1) * d].T
        bs = jnp.dot(kmean_ref[u], qt, precision=lax.Precision.HIGHEST,
                     preferred_element_type=F32)
        cnt, bidx = _topk_rank(bs, nbp, i, 0)
        bias_ref[u] = jnp.where((bidx < i) & (cnt < MOBA_TOPK), 0.0, NEG)
        qtb = (qt * (d ** -0.5 * LOG2E)).astype(BF16)
        s = jnp.dot(kblock(u, i), qtb, preferred_element_type=F32)
        s = jnp.where(kk <= qq, s + alibi_ref[u], NEG)
        m0 = jnp.max(s, axis=0, keepdims=True)
        p = jnp.exp2(s - m0)
        acc0 = jnp.dot(vt_ref[u, i], p.astype(BF16), preferred_element_type=F32)
        qtbs.append(qtb)
        carry0 += [m0, acc0]

    def make_body(width):
        def body(jj, carry):
            js = [jj * width + w for w in range(width)]
            raw = [[jnp.dot(kblock(u, j), qtbs[u], preferred_element_type=F32) for u in heads] for j in js]
            stats, probs = [], []
            for u in heads:
                m = carry[2 * u]
                brows = [bias_ref[u, pl.ds(j, 1), :] - slopes[u] * ((i - j) * blk).astype(F32) for j in js]
                sjs = [raw[w][u] + alibi_ref[u] for w in range(width)]
                m_new = m
                for sj, brow in zip(sjs, brows):
                    m_new = jnp.maximum(m_new, jnp.max(sj, axis=0, keepdims=True) + brow)
                stats.append((m_new, jnp.exp2(m - m_new)))
                probs.append([jnp.exp2(sj - (m_new - brow)).astype(BF16) for sj, brow in zip(sjs, brows)])
            out = []
            for u in heads:
                m_new, alpha = stats[u]
                acc = alpha * carry[2 * u + 1]
                for w, j in enumerate(js):
                    acc = acc + jnp.dot(vt_ref[u, j], probs[u][w], preferred_element_type=F32)
                out += [m_new, acc]
            return tuple(out)
        return body

    fin = lax.fori_loop(0, i // 2, make_body(2), tuple(carry0))
    fin = lax.fori_loop(2 * (i // 2), i, make_body(1), fin)
    for u in heads:
        acc = fin[2 * u + 1]
        o_ref[:, u * d:(u + 1) * d] = (acc[0:d] / acc[d:d + 1]).T.astype(o_ref.dtype)


def _moba_prompt(z3, k3, v3, slopes, n_heads, q_group):
    b, t, _ = z3.shape
    blk, d = MOBA_BLOCK, HEAD_DIM
    nb = t // blk
    nbp = -(-nb // 8) * 8
    hg = _tile(n_heads, MOBA_HEADS_PER_STEP, 1)
    n_groups = n_heads // hg
    whole = pl.BlockSpec((None, t, hg * d), lambda bi, g, i: (bi, 0, g), pipeline_mode=pl.Buffered(1))
    return pl.pallas_call(
        functools.partial(_moba_prompt_kernel, nb=nb, nbp=nbp, hg=hg),
        out_shape=jax.ShapeDtypeStruct((b, t, n_heads * d), BF16),
        grid=(b, n_groups, nb),
        in_specs=[pl.BlockSpec(memory_space=pltpu.SMEM),
                  pl.BlockSpec((None, blk, hg * d), lambda bi, g, i: (bi, i, q_group * n_groups + g)),
                  whole, whole],
        out_specs=pl.BlockSpec((None, blk, hg * d), lambda bi, g, i: (bi, i, g)),
        scratch_shapes=[pltpu.VMEM((hg, nb, d + SUM_ROWS, blk), BF16),
                        pltpu.VMEM((hg, nbp, d), F32), pltpu.VMEM((hg, nbp, blk), F32),
                        pltpu.VMEM((hg, blk, blk), F32)],
        compiler_params=_params("parallel", "parallel", "arbitrary", vmem=VMEM_LIMIT_BIG),
        name="moba_prompt",
    )(slopes, z3, k3, v3)


def _moba_select_kernel(ks_ref, q_ref, o_ref, *, n_heads, nb):
    q = q_ref[...]
    lane = lax.broadcasted_iota(jnp.int32, (q.shape[0], HEAD_DIM), 1)
    for h in range(n_heads):
        kmean = ks_ref[h] * (1.0 / MOBA_BLOCK)
        bs = lax.dot_general(q[:, h * HEAD_DIM:(h + 1) * HEAD_DIM], kmean, _NT,
                             precision=lax.Precision.HIGHEST, preferred_element_type=F32)
        cnt, col = _topk_rank(bs, nb, nb, 1)
        out = jnp.zeros(lane.shape, jnp.int32)
        for r in range(MOBA_TOPK):
            idx = jnp.sum(jnp.where(cnt == r, col, 0).astype(F32), axis=1, keepdims=True)
            out = jnp.where(lane == r, idx.astype(jnp.int32), out)
        o_ref[h] = out


def _moba_select(ksum, z3, q_group):
    b, n_heads, nb, d = ksum.shape
    tq = z3.shape[1]
    assert nb >= MOBA_TOPK and tq <= MOBA_BLOCK
    return pl.pallas_call(
        functools.partial(_moba_select_kernel, n_heads=n_heads, nb=nb),
        out_shape=jax.ShapeDtypeStruct((b, n_heads, tq, HEAD_DIM), jnp.int32),
        grid=(b,),
        in_specs=[pl.BlockSpec((None, n_heads, nb, d), lambda bi: (bi, 0, 0, 0)),
                  pl.BlockSpec((None, tq, n_heads * d), lambda bi: (bi, 0, q_group))],
        out_specs=pl.BlockSpec((None, n_heads, tq, HEAD_DIM), lambda bi: (bi, 0, 0, 0)),
        compiler_params=_params("parallel"),
        name="moba_select",
    )(ksum, z3)


def _moba_sample_kernel(pt_ref, sel_ref, slopes_ref, q_ref, kn_ref, vn_ref, ck_ref, cv_ref, o_ref,
                        kbuf, vbuf, sem, *, layer, n_heads, n_pages, page, ppb, tq):
    b = pl.program_id(0)
    h = pl.program_id(1)
    step = b * n_heads + h
    n_steps = pl.num_programs(0) * n_heads
    npg = MOBA_TOPK * ppb
    slot = step % 2

    def copies(bi, hi, to_slot):
        out = []
        for t in range(tq):
            for n in range(npg):
                blk = sel_ref[((bi * n_heads + hi) * tq + t) * MOBA_TOPK + n // ppb]
                pg = pt_ref[bi * n_pages + blk * ppb + n % ppb]
                out.append(pltpu.make_async_copy(ck_ref.at[layer, pg, :, hi, :],
                                                 kbuf.at[to_slot, t * npg + n], sem.at[to_slot, 0]))
                out.append(pltpu.make_async_copy(cv_ref.at[layer, pg, :, hi, :],
                                                 vbuf.at[to_slot, t * npg + n], sem.at[to_slot, 1]))
        return out

    @pl.when(step == 0)
    def _():
        for cp in copies(b, h, slot):
            cp.start()

    @pl.when(step + 1 < n_steps)
    def _():
        nxt = step + 1
        for cp in copies(nxt // n_heads, nxt % n_heads, 1 - slot):
            cp.start()

    for cp in copies(b, h, slot):
        cp.wait()

    slope = slopes_ref[h]
    past_len = n_pages * page
    rowp = lax.broadcasted_iota(jnp.int32, (page, 1), 0)
    rowq = lax.broadcasted_iota(jnp.int32, (tq, 1), 0)
    kn = kn_ref[...]
    vn = vn_ref[...]

    def body(t, carry):
        q_t = q_ref[pl.ds(t, 1), :] * (HEAD_DIM ** -0.5)
        base = ((b * n_heads + h) * tq + t) * MOBA_TOPK
        scores = []
        for n in range(npg):
            kpos0 = sel_ref[base + n // ppb] * MOBA_BLOCK + (n % ppb) * page
            s = jnp.sum(kbuf[slot, t * npg + n] * q_t, axis=1, keepdims=True)
            dist = (past_len + t - kpos0 - rowp).astype(F32)
            scores.append(s - slope * dist)
        s_own = jnp.sum(kn * q_t, axis=1, keepdims=True)
        s_own = jnp.where(rowq <= t, s_own - slope * (t - rowq).astype(F32), NEG)
        m = jnp.max(s_own, axis=0, keepdims=True)
        for s in scores:
            m = jnp.maximum(m, jnp.max(s, axis=0, keepdims=True))
        p_own = jnp.exp(s_own - m)
        l = jnp.sum(p_own, axis=0, keepdims=True)
        acc = jnp.sum(p_own * vn, axis=0, keepdims=True)
        for n, s in enumerate(scores):
            p = jnp.exp(s - m)
            l = l + jnp.sum(p, axis=0, keepdims=True)
            acc = acc + jnp.sum(p * vbuf[slot, t * npg + n], axis=0, keepdims=True)
        o_ref[pl.ds(t, 1), :] = acc / l
        return carry

    lax.fori_loop(0, tq, body, 0)


def _moba_sample(z3, k3, v3, cache_k, cache_v, layer, pt_flat, sel_flat, slopes, n_pages, q_group):
    b, tq, _ = z3.shape
    _, _, page, n_heads, d = cache_k.shape
    ppb = MOBA_BLOCK // page
    n_bufs = tq * MOBA_TOPK * ppb
    new_rows = pl.BlockSpec((None, tq, d), lambda bi, h, pt, sel: (bi, 0, h))
    return pl.pallas_call(
        functools.partial(_moba_sample_kernel, layer=layer, n_heads=n_heads, n_pages=n_pages,
                          page=page, ppb=ppb, tq=tq),
        out_shape=jax.ShapeDtypeStruct((b, tq, n_heads * d), F32),
        grid_spec=pltpu.PrefetchScalarGridSpec(
            num_scalar_prefetch=2,
            grid=(b, n_heads),
            in_specs=[pl.BlockSpec(memory_space=pltpu.SMEM),
                      pl.BlockSpec((None, tq, d), lambda bi, h, pt, sel: (bi, 0, q_group * n_heads + h)),
                      new_rows, new_rows,
                      pl.BlockSpec(memory_space=pl.ANY), pl.BlockSpec(memory_space=pl.ANY)],
            out_specs=pl.BlockSpec((None, tq, d), lambda bi, h, pt, sel: (bi, 0, h)),
            scratch_shapes=[pltpu.VMEM((2, n_bufs, page, d), F32), pltpu.VMEM((2, n_bufs, page, d), F32),
                            pltpu.SemaphoreType.DMA((2, 2))]),
        compiler_params=_params("arbitrary", "arbitrary"),
        name="moba_sample",
    )(pt_flat, sel_flat, slopes, z3, k3, v3, cache_k, cache_v)


def _pool_kernel(x_ref, halo_ref, g0_ref, g1_ref, w_ref, sc_ref, y_ref, st_ref, ext_ref, a_ref, b_ref,
                 *, tm, group_w, halo_is_state, mxu_dtype):
    i = pl.program_id(1)
    x = x_ref[...]
    h = _rms(x, g0_ref[...])
    if halo_is_state:
        ext_ref[0:POOL_HALO - POOL_STATE, :] = jnp.zeros((POOL_HALO - POOL_STATE, x.shape[1]), F32)
        ext_ref[POOL_HALO - POOL_STATE:POOL_HALO, :] = halo_ref[...]
    else:
        ext_ref[0:POOL_HALO, :] = jnp.where(i > 0, _rms(halo_ref[...], g0_ref[...]), 0.0)
    ext_ref[POOL_HALO:POOL_HALO + tm, :] = h

    r_end = POOL_HALO + tm
    g1c, g2c, g3c = group_w, 2 * group_w, 3 * group_w
    a_ref[8:r_end, :] = ext_ref[8:r_end, :] + ext_ref[7:r_end - 1, :]
    b_ref[16:r_end, g1c:] = a_ref[16:r_end, g1c:] + a_ref[14:r_end - 2, g1c:]
    a_ref[24:r_end, g2c:] = b_ref[24:r_end, g2c:] + b_ref[20:r_end - 4, g2c:]
    sums = [a_ref[POOL_HALO:r_end, 0:g1c], b_ref[POOL_HALO:r_end, g1c:g2c],
            a_ref[POOL_HALO:r_end, g2c:g3c],
            a_ref[POOL_HALO:r_end, g3c:] + a_ref[POOL_HALO - 8:r_end - 8, g3c:]]

    t = i * tm + lax.broadcasted_iota(jnp.int32, (tm, 1), 0)
    ys = []
    for gi, win in enumerate(POOL_WINDOWS):
        cols = slice(gi * group_w, (gi + 1) * group_w)
        acc = sums[gi]
        if halo_is_state:
            mean = acc * (1.0 / win)
        else:
            mean = acc / jnp.minimum(t + 1, win).astype(F32)
        pooled = mean - h[:, cols]
        ys.append(jnp.dot(pooled.astype(mxu_dtype), w_ref[gi].astype(mxu_dtype),
                          preferred_element_type=F32))
    y = jnp.concatenate(ys, axis=1) * sc_ref[...]
    y_ref[...] = x + _rms(y, g1_ref[...])
    st_ref[...] = ext_ref[POOL_HALO + tm - POOL_STATE:POOL_HALO + tm, :]


def _pool(x3, state, g0, g1, w_pool, layer, scale, tm, mxu_dtype):
    b, t, d = x3.shape
    tm = _tile(t, tm, POOL_HALO)
    group_w = d // len(POOL_WINDOWS)
    halo_is_state = state is not None
    if halo_is_state:
        assert t == tm
        halo, halo_spec = state, pl.BlockSpec((None, POOL_STATE, d), lambda bi, i: (bi, 0, 0))
    else:
        per = tm // POOL_HALO
        halo = x3
        halo_spec = pl.BlockSpec((None, POOL_HALO, d), lambda bi, i: (bi, jnp.maximum(i * per - 1, 0), 0))
    vec = pl.BlockSpec((1, d), lambda bi, i: (0, 0))
    return pl.pallas_call(
        functools.partial(_pool_kernel, tm=tm, group_w=group_w, halo_is_state=halo_is_state,
                          mxu_dtype=mxu_dtype),
        out_shape=(jax.ShapeDtypeStruct((b, t, d), F32), jax.ShapeDtypeStruct((b, POOL_STATE, d), F32)),
        grid=(b, t // tm),
        in_specs=[pl.BlockSpec((None, tm, d), lambda bi, i: (bi, i, 0)), halo_spec, vec, vec,
                  pl.BlockSpec((None,) + w_pool.shape[1:], lambda bi, i: (layer, 0, 0, 0)), vec],
        out_specs=(pl.BlockSpec((None, tm, d), lambda bi, i: (bi, i, 0)),
                   pl.BlockSpec((None, POOL_STATE, d), lambda bi, i: (bi, 0, 0))),
        scratch_shapes=[pltpu.VMEM((POOL_HALO + tm, d), F32)] * 3,
        compiler_params=_params("parallel", "arbitrary"),
        name="pool",
    )(x3, halo, g0.reshape(1, d), g1.reshape(1, d), w_pool, scale.reshape(1, d))


def _trunk(x3, ret0, pool0, paged, weights, lg, slopes, *, row_tile, small, side_cache=None):
    w_in, w_out, w_pool_bf, w_pool, pool_scale, w_up, w_down, mlp_w, norm_g = weights
    mlp_w = list(mlp_w)
    b, t, d = x3.shape
    depth = w_up.shape[0]
    n_ret = ret0.shape[2]
    n_even = ret0.shape[0]
    group_w = n_ret * HEAD_DIM
    x = x3.reshape(b * t, d)
    mxu_small = F32 if small else BF16
    new_k, new_v, new_ret, new_pool, ksum_parts = [], [], [], [], []
    for l in range(depth):
        if l % 2 == 0:
            e = l // 2
            z, k_new, v_new = _inproj(x, norm_g[l, 0], w_in, e, group_w, 2 * row_tile, 1024)
            z3 = z.reshape(b, t, -1)
            k3 = k_new.reshape(b, t, group_w)
            v3 = v_new.reshape(b, t, group_w)
            todo = [n for n in (l, l + 1) if n < depth and mlp_w[n] is None]
            o_r, s_new, casts = _retention(z3, ret0[e], lg, n_ret, 256, mxu_small,
                                           F32 if small else BF16,
                                           (w_up, w_down, todo) if todo else None)
            if todo and casts is None:
                casts = [(w_up[n].astype(BF16), w_down[n].astype(BF16)) for n in todo]
            for n, (wu, wd) in zip(todo, casts or []):
                mlp_w[n] = (wu[None], wd[None], 0)
            new_k.append(k3.reshape(b, t, n_ret, HEAD_DIM))
            new_v.append(v3.reshape(b, t, n_ret, HEAD_DIM))
            if paged is None:
                o_b = _moba_prompt(z3, k3, v3, slopes, n_ret, 4)
            else:
                cache_k, cache_v, page_table, ksums = paged
                n_seq, n_pages = page_table.shape
                pt_flat = page_table.reshape(-1)
                sel = _moba_select(ksums[e], z3, 4)
                sel_flat = sel[..., :MOBA_TOPK].reshape(-1)
                o_b = _moba_sample(z3, k3, v3, cache_k, cache_v, e, pt_flat, sel_flat, slopes, n_pages, 4)
            x = _outproj(o_r.reshape(b * t, -1), o_b.reshape(b * t, -1), w_out, e, x, norm_g[l, 1],
                         row_tile)
            new_ret.append(s_new)
        else:
            o = l // 2
            wp = w_pool if small else w_pool_bf
            y3, p_new = _pool(x.reshape(b, t, d), None if pool0 is None else pool0[o],
                              norm_g[l, 0], norm_g[l, 1], wp, o, pool_scale[o], 512, mxu_small)
            x = y3.reshape(b * t, d)
            new_pool.append(p_new)
        side = None
        if side_cache is not None:
            cache_k, page_table = side_cache
            per_layer = depth // n_even
            n_side_pages = page_table.size // per_layer
            side = (cache_k, l // per_layer, page_table.reshape(-1), (l % per_layer) * n_side_pages,
                    n_side_pages)
        x = _mlp(x, norm_g[l, 2], norm_g[l, 3], *mlp_w[l], 2 * row_tile, 1024 if small else 512, side)
        if side is not None:
            x, part = x
            ksum_parts.append(part)
    ksums = None
    if side_cache is not None:
        n_seq = side_cache[1].shape[0]
        per_layer = depth // n_even
        ksums = []
        for e in range(n_even):
            ks = jnp.concatenate(ksum_parts[e * per_layer:(e + 1) * per_layer], axis=0)
            ks = ks.reshape((n_seq, -1) + ks.shape[1:])
            ksums.append(ks.transpose(0, 2, 1, 3))
    return (x.reshape(b, t, d), jnp.stack(new_k), jnp.stack(new_v), jnp.stack(new_ret),
            jnp.stack(new_pool), ksums, mlp_w)


def kernel(x_prompt, x_sample, cache_k, cache_v, state_ret, state_pool, page_table, w_in, w_out, w_pool,
           pool_scale, w_up, w_down, norm_g):
    n_ret = state_ret.shape[2]
    n_moba = cache_k.shape[3]
    assert n_moba == n_ret and w_in.shape[2] == 7 * n_ret * HEAD_DIM
    lg = jnp.log1p(-jnp.exp2(-5.0 - jnp.arange(n_ret, dtype=F32)))
    slopes = jnp.exp2(-8.0 * (jnp.arange(n_moba, dtype=F32) + 1.0) / n_moba)
    mlp_w = [None] * w_up.shape[0]
    shared = (w_in.astype(BF16), w_out.astype(BF16), w_pool.astype(BF16), w_pool, pool_scale,
              w_up, w_down)
    n_even = state_ret.shape[0]
    ret_zero = jnp.zeros((n_even, x_prompt.shape[0], n_ret, HEAD_DIM, HEAD_DIM), F32)

    y_p, k_p, v_p, ret_p, pool_p, ksums, mlp_w = _trunk(
        x_prompt, ret_zero, None, None, shared + (mlp_w, norm_g), lg, slopes,
        row_tile=512, small=False, side_cache=(cache_k, page_table))
    y_s, k_s, v_s, ret_s, pool_s, _, _ = _trunk(
        x_sample, state_ret, state_pool, (cache_k, cache_v, page_table, ksums),
        shared + (mlp_w, norm_g), lg, slopes, row_tile=512, small=True)
    return (y_p, y_s, k_p, v_p, k_s, v_s, ret_p, ret_s, pool_p, pool_s)
```
